```python
import math
import jax
import jax.numpy as jnp
from jax import lax
import numpy as np

D_MODEL = 1024
BATCH = 1
SEQ = 16384
DEPTH = 2
DEC_BATCH = 128
DEC_SEQ = 8
PAST_LEN = 16384
PAGE_SIZE = 128

N_EVEN = (DEPTH + 1) // 2
N_ODD = DEPTH // 2
D_FF = 2816
CONV_W = 4
EPS = 1e-6
W_A = D_MODEL
NB_A = 16
BLK_A = W_A // NB_A
C_A = 8.0
H_B = 8
KV_B = 2
HD_B = 64
WINDOW = 128
N_BUCKETS = 32
MAX_DIST = 128
H_C = 4
DK_C = 128
DV_C = 256
CHUNK_C = 64
ROPE_BASE = 10000.0
H_D = 4
DK_D = 128
DV_D = 256
CHUNK_D = 64

EVEN_IN = 2 * W_A + (H_B + 2 * KV_B) * HD_B
EVEN_OUT = W_A + H_B * HD_B
ODD_IN = H_C * (2 * DK_C + 2 * DV_C) + H_D * (2 * DK_D + 2 * DV_D) + 2 * H_D
ODD_OUT = H_C * DV_C + H_D * DV_D
F32 = jnp.float32

kernel_name = 'hybrid_rglru_swa_retention_mlstm_step'


def _rms(xf):
    return xf * lax.rsqrt(jnp.mean(xf * xf, axis=-1, keepdims=True) + EPS)


def _rmsnorm(x, g):
    return (_rms(x.astype(F32)) * g.astype(F32)).astype(x.dtype)


def _swiglu(x, w_in, w_out):
    gate, up = jnp.split(x @ w_in, 2, axis=-1)
    return (jax.nn.silu(gate) * up) @ w_out


def _causal_conv(x, buf, w, b):
    L = x.shape[1]
    xc = jnp.concatenate([buf.astype(x.dtype), x], axis=1)
    y = b
    for i in range(CONV_W):
        y = y + xc[:, i:i + L] * w[i]
    return y, xc[:, xc.shape[1] - (CONV_W - 1):]


def _lin_comb(left, right):
    a_l, b_l = left
    a_r, b_r = right
    return a_l * a_r, a_r * b_l + b_r


def _rg_lru(x, h0, w_r, b_r, w_i, b_i, lam):
    B, L, _ = x.shape
    xb = x.reshape(B, L, NB_A, BLK_A)
    r = jax.nn.sigmoid((jnp.einsum('blnc,ncd->blnd', xb, w_r).reshape(B, L, W_A) + b_r).astype(F32))
    i = jax.nn.sigmoid((jnp.einsum('blnc,ncd->blnd', xb, w_i).reshape(B, L, W_A) + b_i).astype(F32))
    log_a = C_A * r * jax.nn.log_sigmoid(lam.astype(F32))
    a = jnp.exp(log_a)
    u = jnp.sqrt(-jnp.expm1(2.0 * log_a)) * (i * x.astype(F32))
    u = u.at[:, 0].add(a[:, 0] * h0.astype(F32))
    _, h = lax.associative_scan(_lin_comb, (a, u), axis=1)
    return h, h[:, -1]


def _t5_bucket(dist):
    n = np.maximum(dist, 0)
    max_exact = N_BUCKETS // 2
    large = max_exact + (np.log(np.maximum(n, max_exact) / max_exact)
                         / math.log(MAX_DIST / max_exact) * (N_BUCKETS - max_exact)).astype(np.int32)
    return np.where(n < max_exact, n, np.minimum(large, N_BUCKETS - 1)).astype(np.int32)


def _swa(q, k, v, k_prev, v_prev, p0, qn_g, kn_g, sinks, rel_bias):
    B, L = q.shape[:2]
    G = H_B // KV_B
    q = _rmsnorm(q, qn_g)
    k = _rmsnorm(k, kn_g)
    kc = jnp.concatenate([k_prev.astype(k.dtype), k], axis=1)
    vc = jnp.concatenate([v_prev.astype(v.dtype), v], axis=1)
    Bq = min(WINDOW, L)
    nb = L // Bq
    Kc = Bq + WINDOW
    idx = np.arange(nb)[:, None] * Bq + np.arange(Kc)[None, :]
    kb = kc[:, idx]
    vb = vc[:, idx]
    qb = q.reshape(B, nb, Bq, KV_B, G, HD_B)
    s = jnp.einsum('bnqhgd,bnkhd->bnhgqk', qb, kb).astype(F32) * (HD_B ** -0.5)
    rel = np.arange(Bq)[:, None] + WINDOW - np.arange(Kc)[None, :]
    bias = rel_bias.astype(F32)[_t5_bucket(rel)]
    bias = jnp.transpose(bias, (2, 0, 1)).reshape(KV_B, G, Bq, Kc)
    kpos = p0 - WINDOW + idx
    mask = ((rel >= 0) & (rel < WINDOW))[None] & (kpos >= 0)[:, None, :]
    s = jnp.where(mask[None, :, None, None], s + bias, -jnp.inf)
    sink = sinks.astype(F32).reshape(1, 1, KV_B, G, 1, 1)
    m = jnp.maximum(jnp.max(s, axis=-1, keepdims=True), sink)
    p = jnp.exp(s - m)
    den = jnp.sum(p, axis=-1, keepdims=True) + jnp.exp(sink - m)
    o = jnp.einsum('bnhgqk,bnkhd->bnqhgd', (p / den).astype(v.dtype), vb)
    n_rows = kc.shape[1]
    return o.reshape(B, L, H_B * HD_B), kc[:, n_rows - WINDOW:], vc[:, n_rows - WINDOW:]


def _rotary(x, pos):
    half = x.shape[-1] // 2
    inv = ROPE_BASE ** (-jnp.arange(half, dtype=F32) / half)
    ang = pos[:, None] * inv[None, :]
    cos = jnp.cos(ang)[:, None, :]
    sin = jnp.sin(ang)[:, None, :]
    x1 = x[..., :half]
    x2 = x[..., half:]
    return jnp.concatenate([x1 * cos - x2 * sin, x1 * sin + x2 * cos], axis=-1)


def _retention(q, k, v, S0, p0):
    B, L = q.shape[:2]
    pos = p0 + jnp.arange(L, dtype=F32)
    q = _rotary(q.astype(F32), pos)
    k = _rotary(k.astype(F32), pos) * (DK_C ** -0.5)
    v = v.astype(F32)
    log_g = jnp.log1p(-jnp.exp2(-5.0 - jnp.arange(H_C, dtype=F32)))
    Bc = min(CHUNK_C, L)
    nc = L // Bc
    qc = q.reshape(B, nc, Bc, H_C, DK_C)
    kc = k.reshape(B, nc, Bc, H_C, DK_C)
    vc = v.reshape(B, nc, Bc, H_C, DV_C)
    t = np.arange(Bc)
    diff = t[:, None] - t[None, :]
    dmask = jnp.where(diff >= 0,
                      jnp.exp(np.maximum(diff, 0).astype(np.float32)[None] * log_g[:, None, None]), 0.0)
    scores = jnp.einsum('bcihd,bcjhd->bchij', qc, kc) * dmask
    o_in = jnp.einsum('bchij,bcjhe->bcihe', scores, vc)
    zeta = jnp.exp((Bc - 1 - t).astype(np.float32)[None] * log_g[:, None])
    xi = jnp.exp((t + 1).astype(np.float32)[None] * log_g[:, None])
    U = jnp.einsum('bcjhd,hj,bcjhe->cbhde', kc, zeta, vc)
    g_blk = jnp.exp(Bc * log_g)[:, None, None]

    def step(S, U_c):
        return g_blk * S + U_c, S

    S_T, S_prev = lax.scan(step, S0.astype(F32), U)
    o_x = jnp.einsum('bcihd,hi,cbhde->bcihe', qc, xi, S_prev)
    return (o_in + o_x).reshape(B, L, H_C, DV_C), S_T


def _mlstm(q, k, v, i_pre, f_pre, C0, n0, m0):
    B, L = q.shape[:2]
    Bc = min(CHUNK_D, L)
    nc = L // Bc

    def chunks(a):
        return jnp.moveaxis(a.astype(F32).reshape((B, nc, Bc) + a.shape[2:]), 1, 0)

    qs = chunks(q)
    ks = chunks(k.astype(F32) * (DK_D ** -0.5))
    vs = chunks(v)
    i_s = chunks(i_pre)
    f_s = chunks(jax.nn.log_sigmoid(f_pre.astype(F32)))
    causal = np.tril(np.ones((Bc, Bc), dtype=bool))

    def step(carry, inp):
        C, n, m = carry
        qc, kc, vc, ic, fc = inp
        b = jnp.moveaxis(jnp.cumsum(fc, axis=1), 1, 2)
        ih = jnp.moveaxis(ic, 1, 2)
        dlog = jnp.where(causal, b[..., :, None] - b[..., None, :] + ih[..., None, :], -jnp.inf)
        init_log = b + m[..., None]
        m_t = jnp.maximum(init_log, jnp.max(dlog, axis=-1))
        w = jnp.exp(dlog - m_t[..., None])
        a0 = jnp.exp(init_log - m_t)
        s = jnp.einsum('bihd,bjhd->bhij', qc, kc) * w
        num = (jnp.einsum('bhij,bjhe->bihe', s, vc)
               + jnp.moveaxis(a0, 1, 2)[..., None] * jnp.einsum('bihd,bhde->bihe', qc, C))
        den = jnp.sum(s, axis=-1) + a0 * jnp.einsum('bihd,bhd->bhi', qc, n)
        den = jnp.maximum(jnp.abs(den), jnp.exp(-m_t))
        h = num / jnp.moveaxis(den, 1, 2)[..., None]
        b_end = b[..., -1]
        log_end = b_end[..., None] - b + ih
        m_new = jnp.maximum(b_end + m, jnp.max(log_end, axis=-1))
        w_end = jnp.exp(log_end - m_new[..., None])
        a_end = jnp.exp(b_end + m - m_new)
        C_new = a_end[..., None, None] * C + jnp.einsum('bhj,bjhd,bjhe->bhde', w_end, kc, vc)
        n_new = a_end[..., None] * n + jnp.einsum('bhj,bjhd->bhd', w_end, kc)
        return (C_new, n_new, m_new), h

    (C, n, m), hs = lax.scan(step, (C0.astype(F32), n0.astype(F32), m0.astype(F32)),
                             (qs, ks, vs, i_s, f_s))
    return jnp.moveaxis(hs, 0, 1).reshape(B, L, H_D, DV_D), C, n, m


def _even_mixer(h, p0, conv_buf, h0, k_prev, v_prev, w_in, w_out, conv_w, conv_b,
                w_r, b_r, w_i, b_i, lam, qk_g, sinks, rel_bias):
    B, L, _ = h.shape
    z = h @ w_in
    o1 = W_A
    o2 = 2 * W_A
    o3 = o2 + H_B * HD_B
    o4 = o3 + KV_B * HD_B
    xa, ga, qb, kb, vb = jnp.split(z, [o1, o2, o3, o4], axis=-1)
    xa, conv_new = _causal_conv(xa, conv_buf, conv_w, conv_b)
    hs, h_last = _rg_lru(xa, h0, w_r, b_r, w_i, b_i, lam)
    ya = hs.astype(h.dtype) * jax.nn.gelu(ga)
    yb, k_new, v_new = _swa(qb.reshape(B, L, H_B, HD_B), kb.reshape(B, L, KV_B, HD_B),
                           vb.reshape(B, L, KV_B, HD_B), k_prev, v_prev, p0,
                           qk_g[0], qk_g[1], sinks, rel_bias)
    y = jnp.concatenate([ya, yb], axis=-1) @ w_out
    return y, conv_new, h_last, k_new, v_new


def _odd_mixer(h, p0, S0, conv_buf, C0, n0, m0, w_in, w_out, conv_w, conv_b, gate_b):
    B, L, _ = h.shape
    z = h @ w_in
    s1 = H_C * DK_C
    s2 = s1 + H_C * DK_C
    s3 = s2 + H_C * DV_C
    s4 = s3 + H_C * DV_C
    s5 = s4 + 2 * H_D * DK_D
    s6 = s5 + H_D * DV_D
    s7 = s6 + H_D * DV_D
    qc, kc, vc, gc, qkd, vd, od, gates = jnp.split(z, [s1, s2, s3, s4, s5, s6, s7], axis=-1)
    oc, S_new = _retention(qc.reshape(B, L, H_C, DK_C), kc.reshape(B, L, H_C, DK_C),
                           vc.reshape(B, L, H_C, DV_C), S0, p0)
    yc = _rms(oc).reshape(B, L, H_C * DV_C).astype(h.dtype) * jax.nn.silu(gc)
    qkd, conv_new = _causal_conv(qkd, conv_buf, conv_w, conv_b)
    qkd = jax.nn.silu(qkd)
    qd, kd = jnp.split(qkd, 2, axis=-1)
    gates = gates.astype(F32) + gate_b.astype(F32)
    hd, C_new, n_new, m_new = _mlstm(qd.reshape(B, L, H_D, DK_D), kd.reshape(B, L, H_D, DK_D),
                                     vd.reshape(B, L, H_D, DV_D), gates[..., :H_D], gates[..., H_D:],
                                     C0, n0, m0)
    yd = hd.reshape(B, L, H_D * DV_D).astype(h.dtype) * jax.nn.sigmoid(od)
    y = jnp.concatenate([yc, yd], axis=-1) @ w_out
    return y, S_new, conv_new, C_new, n_new, m_new


def _trunk(x, p0, states, weights):
    (a_conv, a_h, b_k, b_v, c_S, d_conv, d_C, d_n, d_m) = states
    (norm_g, f1_in, f1_out, f2_in, f2_out, e_in, e_out, a_cw, a_cb, a_wr, a_br, a_wi, a_bi,
     a_lam, b_qk, b_sinks, rel_bias, o_in, o_out, d_cw, d_cb, d_gb) = weights
    new = [[] for _ in range(9)]
    for li in range(DEPTH):
        x = x + 0.5 * _swiglu(_rmsnorm(x, norm_g[li, 0]), f1_in[li], f1_out[li])
        h = _rmsnorm(x, norm_g[li, 1])
        j = li // 2
        if li % 2 == 0:
            y, *st = _even_mixer(h, p0, a_conv[j], a_h[j], b_k[j], b_v[j], e_in[j], e_out[j],
                                 a_cw[j], a_cb[j], a_wr[j], a_br[j], a_wi[j], a_bi[j], a_lam[j],
                                 b_qk[j], b_sinks[j], rel_bias)
            for slot, s in zip(range(0, 4), st):
                new[slot].append(s)
        else:
            y, *st = _odd_mixer(h, p0, c_S[j], d_conv[j], d_C[j], d_n[j], d_m[j], o_in[j], o_out[j],
                                d_cw[j], d_cb[j], d_gb[j])
            for slot, s in zip(range(4, 9), st):
                new[slot].append(s)
        x = x + y
        x = x + 0.5 * _swiglu(_rmsnorm(x, norm_g[li, 2]), f2_in[li], f2_out[li])
    return x, [jnp.stack(s).astype(x.dtype) for s in new]


def setup_inputs(seed: int = 0) -> dict:
    key = jax.random.key(seed)
    ks = iter(jax.random.split(key, 48))

    def nrm(shape, scale):
        return jax.random.normal(next(ks), shape, F32) * scale

    u = jax.random.uniform(next(ks), (N_EVEN, W_A), F32, minval=0.9, maxval=0.999)
    a_base = u ** (1.0 / C_A)
    f_bias = jnp.broadcast_to(jnp.linspace(3.0, 6.0, H_D, dtype=F32), (N_ODD, H_D))
    return {
        'x_prompt': nrm((BATCH, SEQ, D_MODEL), 1.0),
        'x_sample': nrm((DEC_BATCH, DEC_SEQ, D_MODEL), 1.0),
        'state_a_conv': nrm((N_EVEN, DEC_BATCH, CONV_W - 1, W_A), 1.0),
        'state_a_h': nrm((N_EVEN, DEC_BATCH, W_A), 0.5),
        'cache_b_k': nrm((N_EVEN, DEC_BATCH, WINDOW, KV_B, HD_B), 1.0),
        'cache_b_v': nrm((N_EVEN, DEC_BATCH, WINDOW, KV_B, HD_B), 1.0),
        'state_c_S': nrm((N_ODD, DEC_BATCH, H_C, DK_C, DV_C), 0.3),
        'state_d_conv': nrm((N_ODD, DEC_BATCH, CONV_W - 1, 2 * H_D * DK_D), 1.0),
        'state_d_C': nrm((N_ODD, DEC_BATCH, H_D, DK_D, DV_D), 0.3),
        'state_d_n': nrm((N_ODD, DEC_BATCH, H_D, DK_D), 0.3),
        'state_d_m': nrm((N_ODD, DEC_BATCH, H_D), 1.0),
        'norm_g': 1.0 + nrm((DEPTH, 3, D_MODEL), 0.02),
        'ffn1_w_in': nrm((DEPTH, D_MODEL, 2 * D_FF), D_MODEL ** -0.5),
        'ffn1_w_out': nrm((DEPTH, D_FF, D_MODEL), D_FF ** -0.5),
        'ffn2_w_in': nrm((DEPTH, D_MODEL, 2 * D_FF), D_MODEL ** -0.5),
        'ffn2_w_out': nrm((DEPTH, D_FF, D_MODEL), D_FF ** -0.5),
        'even_w_in': nrm((N_EVEN, D_MODEL, EVEN_IN), D_MODEL ** -0.5),
        'even_w_out': nrm((N_EVEN, EVEN_OUT, D_MODEL), EVEN_OUT ** -0.5),
        'a_conv_w': nrm((N_EVEN, CONV_W, W_A), CONV_W ** -0.5),
        'a_conv_b': nrm((N_EVEN, W_A), 0.01),
        'a_w_r': nrm((N_EVEN, NB_A, BLK_A, BLK_A), BLK_A ** -0.5),
        'a_b_r': nrm((N_EVEN, W_A), 0.01),
        'a_w_i': nrm((N_EVEN, NB_A, BLK_A, BLK_A), BLK_A ** -0.5),
        'a_b_i': nrm((N_EVEN, W_A), 0.01),
        'a_lambda': jnp.log(a_base) - jnp.log1p(-a_base),
        'b_qk_norm': 1.0 + nrm((N_EVEN, 2, HD_B), 0.02),
        'b_sinks': nrm((N_EVEN, H_B), 0.5),
        'rel_bias': nrm((N_BUCKETS, H_B), 0.2),
        'odd_w_in': nrm((N_ODD, D_MODEL, ODD_IN), D_MODEL ** -0.5),
        'odd_w_out': nrm((N_ODD, ODD_OUT, D_MODEL), ODD_OUT ** -0.5),
        'd_conv_w': nrm((N_ODD, CONV_W, 2 * H_D * DK_D), CONV_W ** -0.5),
        'd_conv_b': nrm((N_ODD, 2 * H_D * DK_D), 0.01),
        'd_gate_b': jnp.concatenate([nrm((N_ODD, H_D), 0.1), f_bias + nrm((N_ODD, H_D), 0.01)], axis=-1),
    }


def reference(x_prompt, x_sample, state_a_conv, state_a_h, cache_b_k, cache_b_v, state_c_S,
              state_d_conv, state_d_C, state_d_n, state_d_m, norm_g, ffn1_w_in, ffn1_w_out,
              ffn2_w_in, ffn2_w_out, even_w_in, even_w_out, a_conv_w, a_conv_b, a_w_r, a_b_r,
              a_w_i, a_b_i, a_lambda, b_qk_norm, b_sinks, rel_bias, odd_w_in, odd_w_out,
              d_conv_w, d_conv_b, d_gate_b):
    weights = (norm_g, ffn1_w_in, ffn1_w_out, ffn2_w_in, ffn2_w_out, even_w_in, even_w_out,
               a_conv_w, a_conv_b, a_w_r, a_b_r, a_w_i, a_b_i, a_lambda, b_qk_norm, b_sinks,
               rel_bias, odd_w_in, odd_w_out, d_conv_w, d_conv_b, d_gate_b)
    dt = x_prompt.dtype
    prompt_init = (
        jnp.zeros((N_EVEN, BATCH, CONV_W - 1, W_A), dt),
        jnp.zeros((N_EVEN, BATCH, W_A), dt),
        jnp.zeros((N_EVEN, BATCH, WINDOW, KV_B, HD_B), dt),
        jnp.zeros((N_EVEN, BATCH, WINDOW, KV_B, HD_B), dt),
        jnp.zeros((N_ODD, BATCH, H_C, DK_C, DV_C), dt),
        jnp.zeros((N_ODD, BATCH, CONV_W - 1, 2 * H_D * DK_D), dt),
        jnp.zeros((N_ODD, BATCH, H_D, DK_D, DV_D), dt),
        jnp.zeros((N_ODD, BATCH, H_D, DK_D), dt),
        jnp.zeros((N_ODD, BATCH, H_D), dt),
    )
    y_prompt, (p_a_conv, p_a_h, p_b_k, p_b_v, p_c_S, p_d_conv, p_d_C, p_d_n, p_d_m) = _trunk(
        x_prompt, 0, prompt_init, weights)
    sample_init = (state_a_conv, state_a_h, cache_b_k, cache_b_v, state_c_S, state_d_conv,
                   state_d_C, state_d_n, state_d_m)
    y_sample, (s_a_conv, s_a_h, s_b_k, s_b_v, s_c_S, s_d_conv, s_d_C, s_d_n, s_d_m) = _trunk(
        x_sample, PAST_LEN, sample_init, weights)
    return (y_prompt, y_sample, p_a_conv, s_a_conv, p_a_h, s_a_h, p_b_k, s_b_k, p_b_v, s_b_v,
            p_c_S, s_c_S, p_d_conv, s_d_conv, p_d_C, s_d_C, p_d_n, s_d_n, p_d_m, s_d_m)
```

```python
import functools
import math

import jax
import jax.numpy as jnp
import numpy as np
from jax import lax
from jax.experimental import pallas as pl
from jax.experimental.pallas import tpu as pltpu

F32 = jnp.float32
BF16 = jnp.bfloat16

PAST_LEN = 16384
EPS = 1e-6
CONV_W = 4
C_A = 8.0
NB_A = 16
H_B, KV_B, HD_B = 8, 2, 64
G_B = H_B // KV_B
WINDOW = 128
N_BUCKETS = 32
MAX_DIST = 128
H_C, DK_C, DV_C = 4, 128, 256
H_D, DK_D, DV_D = 4, 128, 256
ROPE_BASE = 10000.0
NEG = -1e30

LANES = 128
SUBLANES = 8
VMEM_LIMIT = 56 * 1024 * 1024

ROW_TILE = 512
MIX_TILE = 256
FF_CHUNK = 256
EVEN_SB = 32
ODD_SB = 8


def _params(n_grid_dims=1):
    return pltpu.CompilerParams(dimension_semantics=("arbitrary",) * n_grid_dims,
                                vmem_limit_bytes=VMEM_LIMIT)


def _resident(shape):
    nd = len(shape)
    return pl.BlockSpec(shape, lambda i, _nd=nd: (0,) * _nd, pipeline_mode=pl.Buffered(1))


def _dot(a, b):
    return jnp.dot(a.astype(BF16), b.astype(BF16), preferred_element_type=F32)


def _dot_nt(a, b):
    return lax.dot_general(a.astype(BF16), b.astype(BF16), (((1,), (1,)), ((), ())),
                           preferred_element_type=F32)


def _dot_tn(a, b):
    return lax.dot_general(a.astype(BF16), b.astype(BF16), (((0,), (0,)), ((), ())),
                           preferred_element_type=F32)


def _rms_scale(x):
    return lax.rsqrt(jnp.mean(x * x, axis=-1, keepdims=True) + EPS)


def _sigmoid(x):
    return 1.0 / (1.0 + jnp.exp(-x))


def _silu(x):
    return x * _sigmoid(x)


def _log_sigmoid(x):
    return jnp.minimum(x, 0.0) - jnp.log1p(jnp.exp(-jnp.abs(x)))


def _gelu_tanh(x):
    return 0.5 * x * (1.0 + jnp.tanh(math.sqrt(2.0 / math.pi) * (x + 0.044715 * (x * x * x))))


def _ffn_body(x_ref, g_ref, win_ref, wout_ref, o_ref, acc_ref, *, d_ff, chunk):
    x = x_ref[...]
    nb = (x * _rms_scale(x) * g_ref[...]).astype(BF16)
    acc_ref[...] = jnp.zeros_like(acc_ref)

    def step(j, carry):
        c0 = pl.multiple_of(j * chunk, chunk)
        gate = jnp.dot(nb, win_ref[:, pl.ds(c0, chunk)], preferred_element_type=F32)
        up = jnp.dot(nb, win_ref[:, pl.ds(d_ff + c0, chunk)], preferred_element_type=F32)
        mid = (_silu(gate) * up).astype(BF16)
        acc_ref[...] += jnp.dot(mid, wout_ref[pl.ds(c0, chunk), :], preferred_element_type=F32)
        return carry

    lax.fori_loop(0, d_ff // chunk, step, 0)
    o_ref[...] = x + 0.5 * acc_ref[...]


def _ffn(x, g, w_in, w_out):
    n, d = x.shape
    d_ff = w_out.shape[0]
    tm = min(ROW_TILE, n)
    assert n % tm == 0 and d_ff % FF_CHUNK == 0
    return pl.pallas_call(
        functools.partial(_ffn_body, d_ff=d_ff, chunk=FF_CHUNK),
        grid=(n // tm,),
        in_specs=[pl.BlockSpec((tm, d), lambda i: (i, 0)), _resident((1, d)),
                  _resident(w_in.shape), _resident(w_out.shape)],
        out_specs=pl.BlockSpec((tm, d), lambda i: (i, 0)),
        out_shape=jax.ShapeDtypeStruct((n, d), F32),
        scratch_shapes=[pltpu.VMEM((tm, d), F32)],
        compiler_params=_params(), name="ffn")(x, g.reshape(1, d), w_in, w_out)


def _norm_proj_body(x_ref, g_ref, w_ref, o_ref):
    x = x_ref[...]
    nb = (x * _rms_scale(x) * g_ref[...]).astype(BF16)
    o_ref[...] = jnp.dot(nb, w_ref[...], preferred_element_type=F32)


def _norm_proj(x, g, w):
    n, d = x.shape
    m = w.shape[1]
    tm = min(MIX_TILE, n)
    assert n % tm == 0
    return pl.pallas_call(
        _norm_proj_body, grid=(n // tm,),
        in_specs=[pl.BlockSpec((tm, d), lambda i: (i, 0)), _resident((1, d)), _resident(w.shape)],
        out_specs=pl.BlockSpec((tm, m), lambda i: (i, 0)),
        out_shape=jax.ShapeDtypeStruct((n, m), F32),
        compiler_params=_params(), name="norm_proj")(x, g.reshape(1, d), w)


def _out_proj_body(y_ref, w_ref, x_ref, o_ref):
    o_ref[...] = x_ref[...] + jnp.dot(y_ref[...], w_ref[...], preferred_element_type=F32)


def _out_proj(y, w, x):
    n, k = y.shape
    d = w.shape[1]
    tm = min(ROW_TILE, n)
    assert n % tm == 0
    return pl.pallas_call(
        _out_proj_body, grid=(n // tm,),
        in_specs=[pl.BlockSpec((tm, k), lambda i: (i, 0)), _resident(w.shape),
                  pl.BlockSpec((tm, d), lambda i: (i, 0))],
        out_specs=pl.BlockSpec((tm, d), lambda i: (i, 0)),
        out_shape=jax.ShapeDtypeStruct((n, d), F32),
        compiler_params=_params(), name="out_proj")(y, w, x)


def _rglru_gates(y, wri_ref, bri_ref, lam_ref):
    w = y.shape[1]
    gw = wri_ref.shape[1]
    yb = y.astype(BF16)
    logsig = _log_sigmoid(lam_ref[...])
    a_parts, u_parts = [], []
    for g in range(w // gw):
        cols = slice(g * gw, (g + 1) * gw)
        ri = jnp.dot(yb[:, cols], wri_ref[g], preferred_element_type=F32)
        r = _sigmoid(ri[:, :gw] + bri_ref[0:1, cols])
        i = _sigmoid(ri[:, gw:] + bri_ref[1:2, cols])
        log_a = C_A * r * logsig[:, cols]
        th = jnp.tanh(log_a)
        one_minus_a2 = -2.0 * th / (1.0 - th)
        a_parts.append(jnp.exp(log_a))
        u_parts.append(jnp.sqrt(one_minus_a2) * (i * y[:, cols]))
    return jnp.concatenate(a_parts, axis=1), jnp.concatenate(u_parts, axis=1)


def _group_scan(a3, u3):
    t = lax.broadcasted_iota(jnp.int32, a3.shape, 1)
    s = 1
    while s < SUBLANES:
        keep = t >= s
        u3 = jnp.where(keep, a3 * pltpu.roll(u3, s, axis=1) + u3, u3)
        a3 = jnp.where(keep, a3 * pltpu.roll(a3, s, axis=1), a3)
        s *= 2
    return a3, u3


def _even_prompt_body(z_ref, cw_ref, cb_ref, wri_ref, bri_ref, lam_ref, qg_ref, kg_ref, bias_ref,
                      sink_ref, y_ref, h_ref, kl_ref, vl_ref, xbuf, hcar, hbuf, kbuf, vbuf, *, tile, w_a):
    step = pl.program_id(0)
    nq = H_B * HD_B
    nkv = KV_B * HD_B

    @pl.when(step == 0)
    def _():
        xbuf[0:SUBLANES, :] = jnp.zeros((SUBLANES, w_a), F32)
        hcar[...] = jnp.zeros_like(hcar)
        kbuf[0:WINDOW, :] = jnp.zeros((WINDOW, nkv), F32)
        vbuf[0:WINDOW, :] = jnp.zeros((WINDOW, nkv), F32)

    xa = z_ref[:, 0:w_a]
    xbuf[SUBLANES:SUBLANES + tile, :] = xa
    cw = cw_ref[...]
    y = cb_ref[...] + cw[3:4] * xa
    for i in range(CONV_W - 1):
        off = SUBLANES - (CONV_W - 1) + i
        y = y + cw[i:i + 1] * xbuf[off:off + tile, :]
    xbuf[0:SUBLANES, :] = xbuf[tile:tile + SUBLANES, :]

    a, u = _rglru_gates(y, wri_ref, bri_ref, lam_ref)
    ng = tile // SUBLANES
    a3, u3 = _group_scan(a.reshape(ng, SUBLANES, w_a), u.reshape(ng, SUBLANES, w_a))
    carry = hcar[0:1, :]
    for g in range(ng):
        hg = u3[g] + a3[g] * carry
        hbuf[g * SUBLANES:(g + 1) * SUBLANES, :] = hg
        carry = hg[SUBLANES - 1:SUBLANES, :]
    hcar[0:1, :] = carry
    h_ref[...] = jnp.broadcast_to(carry, h_ref.shape)
    y_ref[:, 0:w_a] = (hbuf[...] * _gelu_tanh(z_ref[:, w_a:2 * w_a])).astype(BF16)

    k = z_ref[:, 2 * w_a + nq:2 * w_a + nq + nkv]
    k2 = k * k
    lane = lax.broadcasted_iota(jnp.int32, k.shape, 1)
    kscale = jnp.zeros_like(k)
    for hh in range(KV_B):
        ssum = jnp.sum(k2[:, hh * HD_B:(hh + 1) * HD_B], axis=-1, keepdims=True)
        sc = lax.rsqrt(ssum * (1.0 / HD_B) + EPS)
        kscale = jnp.where((lane >= hh * HD_B) & (lane < (hh + 1) * HD_B), sc, kscale)
    kbuf[WINDOW:WINDOW + tile, :] = k * kscale * kg_ref[...]
    vbuf[WINDOW:WINDOW + tile, :] = z_ref[:, 2 * w_a + nq + nkv:2 * w_a + nq + 2 * nkv]

    col = lax.broadcasted_iota(jnp.int32, (G_B * WINDOW, 2 * WINDOW), 1)
    for nb in range(tile // WINDOW):
        r0 = nb * WINDOW
        q = z_ref[r0:r0 + WINDOW, 2 * w_a:2 * w_a + nq]
        outs = []
        for kv in range(KV_B):
            qs = jnp.concatenate([q[:, (kv * G_B + g) * HD_B:(kv * G_B + g + 1) * HD_B]
                                  for g in range(G_B)], axis=0)
            qs = qs * _rms_scale(qs) * qg_ref[...] * (HD_B ** -0.5)
            kk = kbuf[r0:r0 + 2 * WINDOW, kv * HD_B:(kv + 1) * HD_B]
            vv = vbuf[r0:r0 + 2 * WINDOW, kv * HD_B:(kv + 1) * HD_B]
            s = _dot_nt(qs, kk) + bias_ref[kv]
            if nb == 0:
                s = jnp.where(jnp.logical_and(step == 0, col < WINDOW), NEG, s)
            sink = sink_ref[kv]
            m = jnp.maximum(jnp.max(s, axis=-1, keepdims=True), sink)
            p = jnp.exp(s - m)
            den = jnp.sum(p, axis=-1, keepdims=True) + jnp.exp(sink - m)
            o = _dot(p, vv) / den
            outs += [o[g * WINDOW:(g + 1) * WINDOW, :] for g in range(G_B)]
        y_ref[r0:r0 + WINDOW, w_a:w_a + nq] = jnp.concatenate(outs, axis=1).astype(BF16)

    kbuf[0:WINDOW, :] = kbuf[tile:tile + WINDOW, :]
    vbuf[0:WINDOW, :] = vbuf[tile:tile + WINDOW, :]
    kl_ref[...] = kbuf[0:WINDOW, :]
    vl_ref[...] = vbuf[0:WINDOW, :]


def _even_prompt(z, cw, cb, wri, bri, lam, qg, kg, bias, sink):
    n, zin = z.shape
    w_a = cw.shape[1]
    tile = min(MIX_TILE, n)
    assert n % tile == 0 and tile % WINDOW == 0
    nq, nkv = H_B * HD_B, KV_B * HD_B
    outs = pl.pallas_call(
        functools.partial(_even_prompt_body, tile=tile, w_a=w_a),
        grid=(n // tile,),
        in_specs=[pl.BlockSpec((tile, zin), lambda i: (i, 0))] + [_resident(a.shape) for a in
                  (cw, cb, wri, bri, lam, qg, kg, bias, sink)],
        out_specs=[pl.BlockSpec((tile, w_a + nq), lambda i: (i, 0)),
                   pl.BlockSpec((SUBLANES, w_a), lambda i: (0, 0)),
                   pl.BlockSpec((WINDOW, nkv), lambda i: (0, 0)),
                   pl.BlockSpec((WINDOW, nkv), lambda i: (0, 0))],
        out_shape=[jax.ShapeDtypeStruct((n, w_a + nq), BF16),
                   jax.ShapeDtypeStruct((SUBLANES, w_a), F32),
                   jax.ShapeDtypeStruct((WINDOW, nkv), F32),
                   jax.ShapeDtypeStruct((WINDOW, nkv), F32)],
        scratch_shapes=[pltpu.VMEM((tile + SUBLANES, w_a), F32), pltpu.VMEM((SUBLANES, w_a), F32),
                        pltpu.VMEM((tile, w_a), F32),
                        pltpu.VMEM((tile + WINDOW, nkv), F32), pltpu.VMEM((tile + WINDOW, nkv), F32)],
        compiler_params=_params(), name="even_prompt")(z, cw, cb, wri, bri, lam, qg, kg, bias, sink)
    return outs


def _even_sample_body(z_ref, conv_ref, h0_ref, ck_ref, cv_ref, cw_ref, cb_ref, wri_ref, bri_ref, lam_ref,
                      qg_ref, kg_ref, bias_ref, sink_ref, y_ref, hs_ref, ko_ref, vo_ref, xc, *, sb, dl, w_a):
    nq = H_B * HD_B
    nkv = KV_B * HD_B
    rows = sb * dl
    nkeys = WINDOW + dl

    xa3 = z_ref[:, 0:w_a].reshape(sb, dl, w_a)
    xc[:, SUBLANES:SUBLANES + dl, :] = xa3
    xc[:, SUBLANES - (CONV_W - 1):SUBLANES, :] = conv_ref[...]
    cw = cw_ref[...]
    y3 = cb_ref[...] + cw[3:4] * xa3
    for i in range(CONV_W - 1):
        off = SUBLANES - (CONV_W - 1) + i
        y3 = y3 + cw[i:i + 1] * xc[:, off:off + dl, :]
    y = y3.reshape(rows, w_a)

    a, u = _rglru_gates(y, wri_ref, bri_ref, lam_ref)
    a3 = a.reshape(sb, dl, w_a)
    u3 = u.reshape(sb, dl, w_a)
    t = lax.broadcasted_iota(jnp.int32, a3.shape, 1)
    h0 = jnp.broadcast_to(h0_ref[...][:, None, :], a3.shape)
    u3 = jnp.where(t == 0, u3 + a3 * h0, u3)
    _, h3 = _group_scan(a3, u3)
    hs = h3.reshape(rows, w_a)
    hs_ref[...] = hs
    y_ref[:, 0:w_a] = (hs * _gelu_tanh(z_ref[:, w_a:2 * w_a])).astype(BF16)

    q3 = z_ref[:, 2 * w_a:2 * w_a + nq].reshape(sb, dl, nq)
    k3 = z_ref[:, 2 * w_a + nq:2 * w_a + nq + nkv].reshape(sb, dl, nkv)
    v3 = z_ref[:, 2 * w_a + nq + nkv:2 * w_a + nq + 2 * nkv].reshape(sb, dl, nkv)
    kparts = []
    for hh in range(KV_B):
        kh = k3[:, :, hh * HD_B:(hh + 1) * HD_B]
        kparts.append(kh * _rms_scale(kh))
    kn3 = jnp.concatenate(kparts, axis=2) * kg_ref[...]
    ko_ref[:, 0:WINDOW - dl, :] = ck_ref[:, dl:WINDOW, :]
    ko_ref[:, WINDOW - dl:WINDOW, :] = kn3
    vo_ref[:, 0:WINDOW - dl, :] = cv_ref[:, dl:WINDOW, :]
    vo_ref[:, WINDOW - dl:WINDOW, :] = v3

    outs = []
    for kv in range(KV_B):
        hs_ = slice(kv * HD_B, (kv + 1) * HD_B)
        qs = jnp.concatenate([q3[:, :, (kv * G_B + g) * HD_B:(kv * G_B + g + 1) * HD_B]
                              for g in range(G_B)], axis=1)
        qs = qs * _rms_scale(qs) * qg_ref[...] * (HD_B ** -0.5)
        kc = jnp.concatenate([ck_ref[:, :, hs_], kn3[:, :, hs_]], axis=1)
        vc = jnp.concatenate([cv_ref[:, :, hs_], v3[:, :, hs_]], axis=1)
        s = jnp.einsum('bqd,bkd->bqk', qs.astype(BF16), kc.astype(BF16),
                       preferred_element_type=F32) + bias_ref[kv]
        sink = sink_ref[kv]
        m = jnp.maximum(jnp.max(s, axis=-1, keepdims=True), sink)
        p = jnp.exp(s - m)
        den = jnp.sum(p, axis=-1, keepdims=True) + jnp.exp(sink - m)
        o = jnp.einsum('bqk,bkd->bqd', p.astype(BF16), vc.astype(BF16),
                       preferred_element_type=F32) / den
        outs += [o[:, g * dl:(g + 1) * dl, :] for g in range(G_B)]
    y_ref[:, w_a:w_a + nq] = jnp.concatenate(outs, axis=2).reshape(rows, nq).astype(BF16)


def _even_sample(z, conv, h0, ck, cv, cw, cb, wri, bri, lam, qg, kg, bias, sink, *, dl):
    n, zin = z.shape
    db = n // dl
    w_a = cw.shape[1]
    sb = min(EVEN_SB, db)
    assert db % sb == 0 and dl == SUBLANES
    nq, nkv = H_B * HD_B, KV_B * HD_B
    rows = sb * dl
    return pl.pallas_call(
        functools.partial(_even_sample_body, sb=sb, dl=dl, w_a=w_a),
        grid=(db // sb,),
        in_specs=[pl.BlockSpec((rows, zin), lambda i: (i, 0)),
                  pl.BlockSpec((sb, CONV_W - 1, w_a), lambda i: (i, 0, 0)),
                  pl.BlockSpec((sb, w_a), lambda i: (i, 0)),
                  pl.BlockSpec((sb, WINDOW, nkv), lambda i: (i, 0, 0)),
                  pl.BlockSpec((sb, WINDOW, nkv), lambda i: (i, 0, 0))]
                 + [_resident(a.shape) for a in (cw, cb, wri, bri, lam, qg, kg, bias, sink)],
        out_specs=[pl.BlockSpec((rows, w_a + nq), lambda i: (i, 0)),
                   pl.BlockSpec((rows, w_a), lambda i: (i, 0)),
                   pl.BlockSpec((sb, WINDOW, nkv), lambda i: (i, 0, 0)),
                   pl.BlockSpec((sb, WINDOW, nkv), lambda i: (i, 0, 0))],
        out_shape=[jax.ShapeDtypeStruct((n, w_a + nq), BF16),
                   jax.ShapeDtypeStruct((n, w_a), F32),
                   jax.ShapeDtypeStruct((db, WINDOW, nkv), F32),
                   jax.ShapeDtypeStruct((db, WINDOW, nkv), F32)],
        scratch_shapes=[pltpu.VMEM((sb, 2 * SUBLANES, w_a), F32)],
        compiler_params=_params(), name="even_sample")(
            z, conv, h0, ck, cv, cw, cb, wri, bri, lam, qg, kg, bias, sink)


def _retention_log_decay():
    return [float(np.log1p(-np.exp2(np.float32(-5.0 - h)))) for h in range(H_C)]


def _rotate(x, cosf, sinf, axis):
    return x * cosf + pltpu.roll(x, x.shape[axis] // 2, axis=axis) * sinf


def _odd_prompt_body(z_ref, cos_ref, sin_ref, dmask_ref, cw_ref, cb_ref, gb_ref,
                     y_ref, so_ref, co_ref, no_ref, mo_ref, s_s, c_s, n_s, m_s, qkbuf, *, tile):
    step = pl.program_id(0)
    wq = H_C * DK_C
    wv = H_C * DV_C
    o_kc, o_vc, o_gc = wq, 2 * wq, 2 * wq + wv
    o_qk = 2 * wq + 2 * wv
    wqk = 2 * H_D * DK_D
    o_vd = o_qk + wqk
    o_od = o_vd + H_D * DV_D
    o_gt = o_od + H_D * DV_D
    lg = _retention_log_decay()

    @pl.when(step == 0)
    def _():
        s_s[...] = jnp.zeros_like(s_s)
        c_s[...] = jnp.zeros_like(c_s)
        n_s[...] = jnp.zeros_like(n_s)
        m_s[...] = jnp.zeros_like(m_s)
        qkbuf[0:SUBLANES, :] = jnp.zeros((SUBLANES, wqk), F32)

    cosf = cos_ref[...]
    sinf = sin_ref[...]
    ti = lax.broadcasted_iota(jnp.int32, (tile, 1), 0).astype(F32)

    for h in range(H_C):
        q = _rotate(z_ref[:, h * DK_C:(h + 1) * DK_C], cosf, sinf, 1)
        k = _rotate(z_ref[:, o_kc + h * DK_C:o_kc + (h + 1) * DK_C], cosf, sinf, 1) * (DK_C ** -0.5)
        v = z_ref[:, o_vc + h * DV_C:o_vc + (h + 1) * DV_C]
        xi = jnp.exp((ti + 1.0) * lg[h])
        zeta = jnp.exp((tile - 1.0 - ti) * lg[h])
        sc = _dot_nt(q, k) * dmask_ref[h]
        o = _dot(sc, v) + _dot(q * xi, s_s[h])
        s_s[h] = math.exp(tile * lg[h]) * s_s[h] + _dot_tn(k * zeta, v)
        gate = z_ref[:, o_gc + h * DV_C:o_gc + (h + 1) * DV_C]
        y_ref[:, h * DV_C:(h + 1) * DV_C] = (o * _rms_scale(o) * _silu(gate)).astype(BF16)
    so_ref[...] = s_s[...]

    xqk = z_ref[:, o_qk:o_qk + wqk]
    qkbuf[SUBLANES:SUBLANES + tile, :] = xqk
    cw = cw_ref[...]
    qk = cb_ref[...] + cw[3:4] * xqk
    for i in range(CONV_W - 1):
        off = SUBLANES - (CONV_W - 1) + i
        qk = qk + cw[i:i + 1] * qkbuf[off:off + tile, :]
    qkbuf[0:SUBLANES, :] = qkbuf[tile:tile + SUBLANES, :]
    qk = _silu(qk)

    gates = z_ref[:, o_gt:o_gt + LANES] + gb_ref[...]
    logf = _log_sigmoid(gates)
    row = lax.broadcasted_iota(jnp.int32, gates.shape, 0)
    bsum = logf
    s = 1
    while s < tile:
        bsum = bsum + jnp.where(row >= s, pltpu.roll(bsum, s, axis=0), 0.0)
        s *= 2
    bsum_t = bsum.T
    gates_t = gates.T
    ii = lax.broadcasted_iota(jnp.int32, (tile, tile), 0)
    jj = lax.broadcasted_iota(jnp.int32, (tile, tile), 1)
    causal = jj <= ii
    for h in range(H_D):
        q = qk[:, h * DK_D:(h + 1) * DK_D]
        k = qk[:, H_D * DK_D + h * DK_D:H_D * DK_D + (h + 1) * DK_D] * (DK_D ** -0.5)
        v = z_ref[:, o_vd + h * DV_D:o_vd + (h + 1) * DV_D]
        b_col = bsum[:, H_D + h:H_D + h + 1]
        b_row = bsum_t[H_D + h:H_D + h + 1, :]
        i_col = gates[:, h:h + 1]
        i_row = gates_t[h:h + 1, :]
        m_prev = m_s[h:h + 1, 0:1]
        dlog = jnp.where(causal, b_col - b_row + i_row, NEG)
        init_log = b_col + m_prev
        m_t = jnp.maximum(init_log, jnp.max(dlog, axis=-1, keepdims=True))
        w = jnp.exp(dlog - m_t)
        a0 = jnp.exp(init_log - m_t)
        sc = _dot_nt(q, k) * w
        num = _dot(sc, v) + a0 * _dot(q, c_s[h])
        den = jnp.sum(sc, axis=-1, keepdims=True) + a0 * jnp.sum(q * n_s[h:h + 1, :], axis=-1, keepdims=True)
        den = jnp.maximum(jnp.abs(den), jnp.exp(-m_t))
        hd = num / den
        og = z_ref[:, o_od + h * DV_D:o_od + (h + 1) * DV_D]
        y_ref[:, wv + h * DV_D:wv + (h + 1) * DV_D] = (hd * _sigmoid(og)).astype(BF16)
        b_end = b_col[tile - 1:tile, :]
        log_end_col = b_end - b_col + i_col
        m_new = jnp.maximum(b_end + m_prev, jnp.max(log_end_col, axis=0, keepdims=True))
        w_end = jnp.exp(log_end_col - m_new)
        a_end = jnp.exp(b_end + m_prev - m_new)
        kw = k * w_end
        c_s[h] = a_end * c_s[h] + _dot_tn(kw, v)
        n_s[h:h + 1, :] = a_end * n_s[h:h + 1, :] + jnp.sum(kw, axis=0, keepdims=True)
        m_s[h:h + 1, :] = jnp.broadcast_to(m_new, (1, LANES))
    co_ref[...] = c_s[...]
    no_ref[...] = n_s[...]
    mo_ref[...] = m_s[...]


def _odd_prompt(z, cosf, sinf, dmask, cw, cb, gb):
    n, zin = z.shape
    tile = min(MIX_TILE, n)
    assert n % tile == 0 and dmask.shape[1] == tile
    wy = H_C * DV_C + H_D * DV_D
    wqk = 2 * H_D * DK_D
    return pl.pallas_call(
        functools.partial(_odd_prompt_body, tile=tile),
        grid=(n // tile,),
        in_specs=[pl.BlockSpec((tile, zin), lambda i: (i, 0)),
                  pl.BlockSpec((tile, DK_C), lambda i: (i, 0)),
                  pl.BlockSpec((tile, DK_C), lambda i: (i, 0))]
                 + [_resident(a.shape) for a in (dmask, cw, cb, gb)],
        out_specs=[pl.BlockSpec((tile, wy), lambda i: (i, 0)),
                   pl.BlockSpec((H_C, DK_C, DV_C), lambda i: (0, 0, 0)),
                   pl.BlockSpec((H_D, DK_D, DV_D), lambda i: (0, 0, 0)),
                   pl.BlockSpec((SUBLANES, DK_D), lambda i: (0, 0)),
                   pl.BlockSpec((SUBLANES, LANES), lambda i: (0, 0))],
        out_shape=[jax.ShapeDtypeStruct((n, wy), BF16),
                   jax.ShapeDtypeStruct((H_C, DK_C, DV_C), F32),
                   jax.ShapeDtypeStruct((H_D, DK_D, DV_D), F32),
                   jax.ShapeDtypeStruct((SUBLANES, DK_D), F32),
                   jax.ShapeDtypeStruct((SUBLANES, LANES), F32)],
        scratch_shapes=[pltpu.VMEM((H_C, DK_C, DV_C), F32), pltpu.VMEM((H_D, DK_D, DV_D), F32),
                        pltpu.VMEM((SUBLANES, DK_D), F32), pltpu.VMEM((SUBLANES, LANES), F32),
                        pltpu.VMEM((tile + SUBLANES, wqk), F32)],
        compiler_params=_params(), name="odd_prompt")(z, cosf, sinf, dmask, cw, cb, gb)


def _odd_sample_body(z_ref, cos_ref, sin_ref, s_ref, conv_ref, c_ref, n_ref, m_ref, cw_ref, cb_ref, gb_ref,
                     y_ref, so_ref, co_ref, no_ref, mo_ref, xc, *, sb, dl):
    wq = H_C * DK_C
    wv = H_C * DV_C
    o_kc, o_vc, o_gc = wq, 2 * wq, 2 * wq + wv
    o_qk = 2 * wq + 2 * wv
    wqk = 2 * H_D * DK_D
    o_vd = o_qk + wqk
    o_od = o_vd + H_D * DV_D
    o_gt = o_od + H_D * DV_D
    lg = _retention_log_decay()
    rows = sb * dl

    def z3(c0, width):
        return z_ref[:, c0:c0 + width].reshape(sb, dl, width)

    cosf = cos_ref[...][None]
    sinf = sin_ref[...][None]
    t1 = lax.broadcasted_iota(jnp.int32, (1, dl, 1), 1)
    tf = t1.astype(F32)

    for h in range(H_C):
        q = _rotate(z3(h * DK_C, DK_C), cosf, sinf, 2)
        k = _rotate(z3(o_kc + h * DK_C, DK_C), cosf, sinf, 2) * (DK_C ** -0.5)
        v = z3(o_vc + h * DV_C, DV_C)
        xi = jnp.exp((tf + 1.0) * lg[h])
        zeta = jnp.exp((dl - 1.0 - tf) * lg[h])
        s0 = s_ref[:, h]
        o = jnp.einsum('bqd,bde->bqe', (q * xi).astype(BF16), s0.astype(BF16), preferred_element_type=F32)
        for s in range(dl):
            ks = k if s == 0 else pltpu.roll(k, s, axis=1)
            vs = v if s == 0 else pltpu.roll(v, s, axis=1)
            coef = jnp.sum(q * ks, axis=-1, keepdims=True) * math.exp(s * lg[h])
            o = o + jnp.where(t1 >= s, coef, 0.0) * vs
        upd = jnp.einsum('btd,bte->bde', (k * zeta).astype(BF16), v.astype(BF16), preferred_element_type=F32)
        so_ref[:, h] = math.exp(dl * lg[h]) * s0 + upd
        gate = z3(o_gc + h * DV_C, DV_C)
        y_ref[:, h * DV_C:(h + 1) * DV_C] = (o * _rms_scale(o) * _silu(gate)).reshape(rows, DV_C).astype(BF16)

    xqk3 = z3(o_qk, wqk)
    xc[:, SUBLANES:SUBLANES + dl, :] = xqk3
    xc[:, SUBLANES - (CONV_W - 1):SUBLANES, :] = conv_ref[...]
    cw = cw_ref[...]
    qk = cb_ref[...] + cw[3:4] * xqk3
    for i in range(CONV_W - 1):
        off = SUBLANES - (CONV_W - 1) + i
        qk = qk + cw[i:i + 1] * xc[:, off:off + dl, :]
    qk = _silu(qk)

    gates = z3(o_gt, LANES) + gb_ref[...]
    logf = _log_sigmoid(gates)
    t = lax.broadcasted_iota(jnp.int32, gates.shape, 1)
    bsum = logf
    s = 1
    while s < dl:
        bsum = bsum + jnp.where(t >= s, pltpu.roll(bsum, s, axis=1), 0.0)
        s *= 2
    bsum = pltpu.roll(bsum, LANES - H_D, axis=2)
    m0 = m_ref[...]
    init_log = bsum + m0
    m_t = init_log
    dlogs = []
    for s in range(dl):
        if s == 0:
            d = gates
        else:
            d = bsum - pltpu.roll(bsum, s, axis=1) + pltpu.roll(gates, s, axis=1)
        d = jnp.where(t >= s, d, NEG)
        dlogs.append(d)
        m_t = jnp.maximum(m_t, d)
    a0 = jnp.exp(init_log - m_t)
    ws = [jnp.exp(d - m_t) for d in dlogs]
    inv_floor = jnp.exp(-m_t)
    b_end = bsum[:, dl - 1:dl, :]
    log_end = b_end - bsum + gates
    m_new = jnp.maximum(b_end + m0, jnp.max(log_end, axis=1, keepdims=True))
    w_end = jnp.exp(log_end - m_new)
    a_end = jnp.exp(b_end + m0 - m_new)
    mo_ref[...] = m_new
    for h in range(H_D):
        q = qk[:, :, h * DK_D:(h + 1) * DK_D]
        k = qk[:, :, H_D * DK_D + h * DK_D:H_D * DK_D + (h + 1) * DK_D] * (DK_D ** -0.5)
        v = z3(o_vd + h * DV_D, DV_D)
        c0 = c_ref[:, h]
        n0 = n_ref[:, h:h + 1, :]
        a0h = a0[:, :, h:h + 1]
        num = a0h * jnp.einsum('bqd,bde->bqe', q.astype(BF16), c0.astype(BF16), preferred_element_type=F32)
        den = a0h * jnp.sum(q * n0, axis=-1, keepdims=True)
        for s in range(dl):
            ks = k if s == 0 else pltpu.roll(k, s, axis=1)
            vs = v if s == 0 else pltpu.roll(v, s, axis=1)
            coef = jnp.sum(q * ks, axis=-1, keepdims=True) * ws[s][:, :, h:h + 1]
            num = num + coef * vs
            den = den + coef
        den = jnp.maximum(jnp.abs(den), inv_floor[:, :, h:h + 1])
        og = z3(o_od + h * DV_D, DV_D)
        y_ref[:, wv + h * DV_D:wv + (h + 1) * DV_D] = (
            (num / den) * _sigmoid(og)).reshape(rows, DV_D).astype(BF16)
        kw = k * w_end[:, :, h:h + 1]
        aeh = a_end[:, :, h:h + 1]
        upd = jnp.einsum('btd,bte->bde', kw.astype(BF16), v.astype(BF16), preferred_element_type=F32)
        co_ref[:, h] = aeh * c0 + upd
        no_ref[:, h:h + 1, :] = aeh * n0 + jnp.sum(kw, axis=1, keepdims=True)


def _odd_sample(z, cosf, sinf, s0, conv, c0, n0, m0, cw, cb, gb, *, dl):
    n, zin = z.shape
    db = n // dl
    sb = min(ODD_SB, db)
    assert db % sb == 0 and dl == SUBLANES
    rows = sb * dl
    wy = H_C * DV_C + H_D * DV_D
    wqk = 2 * H_D * DK_D
    st_spec = pl.BlockSpec((sb, H_C, DK_C, DV_C), lambda i: (i, 0, 0, 0))
    n_spec = pl.BlockSpec((sb, H_D, DK_D), lambda i: (i, 0, 0))
    m_spec = pl.BlockSpec((sb, 1, LANES), lambda i: (i, 0, 0))
    return pl.pallas_call(
        functools.partial(_odd_sample_body, sb=sb, dl=dl),
        grid=(db // sb,),
        in_specs=[pl.BlockSpec((rows, zin), lambda i: (i, 0)), _resident(cosf.shape), _resident(sinf.shape),
                  st_spec, pl.BlockSpec((sb, CONV_W - 1, wqk), lambda i: (i, 0, 0)), st_spec, n_spec, m_spec]
                 + [_resident(a.shape) for a in (cw, cb, gb)],
        out_specs=[pl.BlockSpec((rows, wy), lambda i: (i, 0)), st_spec, st_spec, n_spec, m_spec],
        out_shape=[jax.ShapeDtypeStruct((n, wy), BF16),
                   jax.ShapeDtypeStruct(s0.shape, F32), jax.ShapeDtypeStruct(c0.shape, F32),
                   jax.ShapeDtypeStruct(n0.shape, F32), jax.ShapeDtypeStruct(m0.shape, F32)],
        scratch_shapes=[pltpu.VMEM((sb, 2 * SUBLANES, wqk), F32)],
        compiler_params=_params(), name="odd_sample")(z, cosf, sinf, s0, conv, c0, n0, m0, cw, cb, gb)


def _t5_bucket(dist):
    n = np.maximum(dist, 0)
    max_exact = N_BUCKETS // 2
    large = max_exact + (np.log(np.maximum(n, max_exact) / max_exact)
                         / math.log(MAX_DIST / max_exact) * (N_BUCKETS - max_exact)).astype(np.int32)
    return np.where(n < max_exact, n, np.minimum(large, N_BUCKETS - 1)).astype(np.int32)


def _attn_bias(rel_bias, nq_rows, nkeys, p0):
    rel = np.arange(nq_rows)[:, None] + WINDOW - np.arange(nkeys)[None, :]
    kpos_ok = (p0 - WINDOW + np.arange(nkeys)) >= 0
    mask = (rel >= 0) & (rel < WINDOW) & kpos_ok[None, :]
    bias = rel_bias.astype(F32)[_t5_bucket(rel)]
    bias = jnp.where(mask[:, :, None], bias, NEG)
    bias = jnp.transpose(bias, (2, 0, 1)).reshape(KV_B, G_B * nq_rows, nkeys)
    return bias


def _sink_rows(sinks, nq_rows):
    return jnp.repeat(sinks.astype(F32).reshape(KV_B, G_B), nq_rows, axis=1).reshape(KV_B, G_B * nq_rows, 1)


def _rope_tables(p0, n):
    half = DK_C // 2
    pos = p0 + jnp.arange(n, dtype=F32)
    inv = ROPE_BASE ** (-jnp.arange(half, dtype=F32) / half)
    ang = pos[:, None] * inv[None, :]
    cos, sin = jnp.cos(ang), jnp.sin(ang)
    return jnp.concatenate([cos, cos], axis=1), jnp.concatenate([-sin, sin], axis=1)


def _decay_mask(tile):
    t = np.arange(tile)
    diff = (t[:, None] - t[None, :]).astype(np.float32)
    lg = np.asarray(_retention_log_decay(), np.float32)
    return jnp.asarray(np.where(diff >= 0, np.exp(np.maximum(diff, 0)[None] * lg[:, None, None]), 0.0), F32)


def _block_diag_gates(w_r, w_i, group):
    nb, c, _ = w_r.shape
    eye = jnp.eye(group, dtype=w_r.dtype)

    def bd(w):
        w4 = w.reshape(nb // group, group, c, c)
        return jnp.einsum('gncd,nm->gncmd', w4, eye).reshape(nb // group, group * c, group * c)

    return jnp.concatenate([bd(w_r), bd(w_i)], axis=2).astype(BF16)


def kernel(x_prompt, x_sample, state_a_conv, state_a_h, cache_b_k, cache_b_v, state_c_S, state_d_conv, state_d_C, state_d_n, state_d_m, norm_g, ffn1_w_in, ffn1_w_out, ffn2_w_in, ffn2_w_out, even_w_in, even_w_out, a_conv_w, a_conv_b, a_w_r, a_b_r, a_w_i, a_b_i, a_lambda, b_qk_norm, b_sinks, rel_bias, odd_w_in, odd_w_out, d_conv_w, d_conv_b, d_gate_b):
    bp, lp, d = x_prompt.shape
    db, dl, _ = x_sample.shape
    assert bp == 1, "prompt group is a single sequence"
    depth = norm_g.shape[0]
    w_a = a_conv_w.shape[2]
    nkv = KV_B * HD_B
    wqk = 2 * H_D * DK_D
    tile = min(MIX_TILE, lp)

    xp = x_prompt.reshape(lp, d)
    xs = x_sample.reshape(db * dl, d)

    bias_p = _attn_bias(rel_bias, WINDOW, 2 * WINDOW, WINDOW)
    bias_s = _attn_bias(rel_bias, dl, WINDOW + dl, PAST_LEN)
    cos_p, sin_p = _rope_tables(0.0, lp)
    cos_s, sin_s = _rope_tables(float(PAST_LEN), dl)
    dmask = _decay_mask(tile)

    st = {}
    for li in range(depth):
        j = li // 2
        w1i, w1o = ffn1_w_in[li].astype(BF16), ffn1_w_out[li].astype(BF16)
        xp = _ffn(xp, norm_g[li, 0], w1i, w1o)
        xs = _ffn(xs, norm_g[li, 0], w1i, w1o)
        if li % 2 == 0:
            w_in = even_w_in[j].astype(BF16)
            w_out = even_w_out[j].astype(BF16)
            zp = _norm_proj(xp, norm_g[li, 1], w_in)
            zs = _norm_proj(xs, norm_g[li, 1], w_in)
            shared = (a_conv_w[j], a_conv_b[j].reshape(1, w_a),
                      _block_diag_gates(a_w_r[j], a_w_i[j], 4),
                      jnp.stack([a_b_r[j], a_b_i[j]]), a_lambda[j].reshape(1, w_a),
                      b_qk_norm[j, 0].reshape(1, HD_B), jnp.tile(b_qk_norm[j, 1], KV_B).reshape(1, nkv))
            yp, h_p, k_p, v_p = _even_prompt(zp, *shared, bias_p, _sink_rows(b_sinks[j], WINDOW))
            ys, hs_all, k_s, v_s = _even_sample(
                zs, state_a_conv[j], state_a_h[j], cache_b_k[j].reshape(db, WINDOW, nkv),
                cache_b_v[j].reshape(db, WINDOW, nkv), *shared, bias_s, _sink_rows(b_sinks[j], dl), dl=dl)
            st.setdefault('a_conv', ([], []))
            st['a_conv'][0].append(zp[lp - (CONV_W - 1):, :w_a].reshape(1, CONV_W - 1, w_a))
            st['a_conv'][1].append(zs.reshape(db, dl, -1)[:, dl - (CONV_W - 1):, :w_a])
            st.setdefault('a_h', ([], []))
            st['a_h'][0].append(h_p[0:1])
            st['a_h'][1].append(hs_all.reshape(db, dl, w_a)[:, dl - 1])
            st.setdefault('b_k', ([], []))
            st['b_k'][0].append(k_p.reshape(1, WINDOW, KV_B, HD_B))
            st['b_k'][1].append(k_s.reshape(db, WINDOW, KV_B, HD_B))
            st.setdefault('b_v', ([], []))
            st['b_v'][0].append(v_p.reshape(1, WINDOW, KV_B, HD_B))
            st['b_v'][1].append(v_s.reshape(db, WINDOW, KV_B, HD_B))
        else:
            zin = odd_w_in.shape[2]
            zpad = -(-zin // LANES) * LANES
            w_in = jnp.pad(odd_w_in[j], ((0, 0), (0, zpad - zin))).astype(BF16)
            w_out = odd_w_out[j].astype(BF16)
            zp = _norm_proj(xp, norm_g[li, 1], w_in)
            zs = _norm_proj(xs, norm_g[li, 1], w_in)
            o_qk = 2 * H_C * DK_C + 2 * H_C * DV_C
            gb = jnp.pad(d_gate_b[j], (0, LANES - 2 * H_D)).reshape(1, LANES)
            shared = (d_conv_w[j], d_conv_b[j].reshape(1, wqk), gb)
            yp, s_p, c_p, n_p, m_p = _odd_prompt(zp, cos_p, sin_p, dmask, *shared)
            m0 = jnp.pad(state_d_m[j], ((0, 0), (0, LANES - H_D))).reshape(db, 1, LANES)
            ys, s_s, c_s, n_s, m_s = _odd_sample(zs, cos_s, sin_s, state_c_S[j], state_d_conv[j],
                                                 state_d_C[j], state_d_n[j], m0, *shared, dl=dl)
            st.setdefault('c_S', ([], []))
            st['c_S'][0].append(s_p[None])
            st['c_S'][1].append(s_s)
            st.setdefault('d_conv', ([], []))
            st['d_conv'][0].append(zp[lp - (CONV_W - 1):, o_qk:o_qk + wqk].reshape(1, CONV_W - 1, wqk))
            st['d_conv'][1].append(zs.reshape(db, dl, -1)[:, dl - (CONV_W - 1):, o_qk:o_qk + wqk])
            st.setdefault('d_C', ([], []))
            st['d_C'][0].append(c_p[None])
            st['d_C'][1].append(c_s)
            st.setdefault('d_n', ([], []))
            st['d_n'][0].append(n_p[None, :H_D])
            st['d_n'][1].append(n_s)
            st.setdefault('d_m', ([], []))
            st['d_m'][0].append(m_p[:H_D, 0].reshape(1, H_D))
            st['d_m'][1].append(m_s[:, 0, :H_D])
        xp = _out_proj(yp, w_out, xp)
        xs = _out_proj(ys, w_out, xs)
        w2i, w2o = ffn2_w_in[li].astype(BF16), ffn2_w_out[li].astype(BF16)
        xp = _ffn(xp, norm_g[li, 2], w2i, w2o)
        xs = _ffn(xs, norm_g[li, 2], w2i, w2o)

    outs = [xp.reshape(1, lp, d), xs.reshape(db, dl, d)]
    for name in ('a_conv', 'a_h', 'b_k', 'b_v', 'c_S', 'd_conv', 'd_C', 'd_n', 'd_m'):
        outs.append(jnp.stack(st[name][0]))
        outs.append(jnp.stack(st[name][1]))
    return tuple(outs)
```

```python
import functools
import math

import jax
import jax.numpy as jnp
import numpy as np
from jax import lax
from jax.experimental import pallas as pl
from jax.experimental.pallas import tpu as pltpu

F32 = jnp.float32
BF16 = jnp.bfloat16

PAST_LEN = 16384
EPS = 1e-6
CONV_W = 4
C_A = 8.0
NB_A = 16
H_B, KV_B, HD_B = 8, 2, 64
G_B = H_B // KV_B
WINDOW = 128
N_BUCKETS = 32
MAX_DIST = 128
H_C, DK_C, DV_C = 4, 128, 256
H_D, DK_D, DV_D = 4, 128, 256
ROPE_BASE = 10000.0
NEG = -1e30

LANES = 128
SUBLANES = 8
VMEM_LIMIT = 56 * 1024 * 1024

ROW_TILE = 512
MIX_TILE = 256
FF_CHUNK = 256
EVEN_SB = 32
ODD_SB = 8
PROJ_PIECE = 256


def _params(n_grid_dims=1):
    return pltpu.CompilerParams(dimension_semantics=("arbitrary",) * n_grid_dims,
                                vmem_limit_bytes=VMEM_LIMIT)


def _resident(shape):
    nd = len(shape)
    return pl.BlockSpec(shape, lambda i, _nd=nd: (0,) * _nd, pipeline_mode=pl.Buffered(1))


def _dot(a, b):
    return jnp.dot(a.astype(BF16), b.astype(BF16), preferred_element_type=F32)


def _dot_nt(a, b):
    return lax.dot_general(a.astype(BF16), b.astype(BF16), (((1,), (1,)), ((), ())),
                           preferred_element_type=F32)


def _dot_tn(a, b):
    return lax.dot_general(a.astype(BF16), b.astype(BF16), (((0,), (0,)), ((), ())),
                           preferred_element_type=F32)


def _rms_scale(x):
    return lax.rsqrt(jnp.mean(x * x, axis=-1, keepdims=True) + EPS)


def _sigmoid(x):
    return 1.0 / (1.0 + jnp.exp(-x))


def _silu(x):
    return x * _sigmoid(x)


def _log_sigmoid(x):
    return jnp.minimum(x, 0.0) - jnp.log1p(jnp.exp(-jnp.abs(x)))


def _gelu_tanh(x):
    return 0.5 * x * (1.0 + jnp.tanh(math.sqrt(2.0 / math.pi) * (x + 0.044715 * (x * x * x))))


def _ffn_body(x_ref, g_ref, win_ref, wout_ref, o_ref, *, d_ff, chunk):
    x = x_ref[...]
    nb = (x * _rms_scale(x) * g_ref[...]).astype(BF16)
    acc = None
    for c0 in range(0, d_ff, chunk):
        gate = jnp.dot(nb, win_ref[:, c0:c0 + chunk], preferred_element_type=F32)
        up = jnp.dot(nb, win_ref[:, d_ff + c0:d_ff + c0 + chunk], preferred_element_type=F32)
        mid = (_silu(gate) * up).astype(BF16)
        part = jnp.dot(mid, wout_ref[c0:c0 + chunk, :], preferred_element_type=F32)
        acc = part if acc is None else acc + part
    o_ref[...] = x + 0.5 * acc


def _ffn(x, g, w_in, w_out):
    n, d = x.shape
    d_ff = w_out.shape[0]
    tm = min(ROW_TILE, n)
    assert n % tm == 0 and d_ff % FF_CHUNK == 0
    return pl.pallas_call(
        functools.partial(_ffn_body, d_ff=d_ff, chunk=FF_CHUNK),
        grid=(n // tm,),
        in_specs=[pl.BlockSpec((tm, d), lambda i: (i, 0)), _resident((1, d)),
                  _resident(w_in.shape), _resident(w_out.shape)],
        out_specs=pl.BlockSpec((tm, d), lambda i: (i, 0)),
        out_shape=jax.ShapeDtypeStruct((n, d), F32),
        compiler_params=_params(), name="ffn")(x, g.reshape(1, d), w_in, w_out)


def _norm_proj_body(x_ref, g_ref, w_ref, o_ref):
    x = x_ref[...]
    nb = (x * _rms_scale(x) * g_ref[...]).astype(BF16)
    o_ref[...] = jnp.dot(nb, w_ref[...], preferred_element_type=F32)


def _norm_proj(x, g, w):
    n, d = x.shape
    m = w.shape[1]
    tm = min(MIX_TILE, n)
    assert n % tm == 0
    return pl.pallas_call(
        _norm_proj_body, grid=(n // tm,),
        in_specs=[pl.BlockSpec((tm, d), lambda i: (i, 0)), _resident((1, d)), _resident(w.shape)],
        out_specs=pl.BlockSpec((tm, m), lambda i: (i, 0)),
        out_shape=jax.ShapeDtypeStruct((n, m), F32),
        compiler_params=_params(), name="norm_proj")(x, g.reshape(1, d), w)


def _out_proj_body(y_ref, w_ref, x_ref, o_ref):
    o_ref[...] = x_ref[...] + jnp.dot(y_ref[...], w_ref[...], preferred_element_type=F32)


def _out_proj(y, w, x):
    n, k = y.shape
    d = w.shape[1]
    tm = min(ROW_TILE, n)
    assert n % tm == 0
    return pl.pallas_call(
        _out_proj_body, grid=(n // tm,),
        in_specs=[pl.BlockSpec((tm, k), lambda i: (i, 0)), _resident(w.shape),
                  pl.BlockSpec((tm, d), lambda i: (i, 0))],
        out_specs=pl.BlockSpec((tm, d), lambda i: (i, 0)),
        out_shape=jax.ShapeDtypeStruct((n, d), F32),
        compiler_params=_params(), name="out_proj")(y, w, x)


class _SideWork:
    def __init__(self, pieces=(), stages=1):
        self.pieces, self.stages, self.done, self.stage = list(pieces), stages, 0, 0

    def tick(self):
        self.stage += 1
        upto = min(len(self.pieces), -(-len(self.pieces) * self.stage // self.stages))
        while self.done < upto:
            self.pieces[self.done]()
            self.done += 1

    def flush(self):
        self.stage = self.stages - 1
        self.tick()


def _norm_piece(x_ref, r0, r1, g_ref, nb_ref):
    x = x_ref[r0:r1, :]
    nb_ref[...] = (x * _rms_scale(x) * g_ref[...]).astype(BF16)


def _proj_piece(nb_ref, w_ref, z_ref, c0, c1):
    z_ref[:, c0:c1] = jnp.dot(nb_ref[...], w_ref[:, c0:c1], preferred_element_type=F32)


def _proj_pieces(x_ref, r0, r1, g_ref, nb_ref, w_ref, z_ref, width):
    m = w_ref.shape[1]
    return [functools.partial(_norm_piece, x_ref, r0, r1, g_ref, nb_ref)] + [
        functools.partial(_proj_piece, nb_ref, w_ref, z_ref, c0, min(c0 + width, m))
        for c0 in range(0, m, width)]


def _out_piece(y_ref, w_ref, x_ref, o_ref, r0, r1, c0, c1):
    o_ref[r0:r1, c0:c1] = x_ref[r0:r1, c0:c1] + jnp.dot(y_ref[...], w_ref[:, c0:c1],
                                                         preferred_element_type=F32)


def _out_pieces(y_ref, w_ref, x_ref, o_ref, r0, r1, width):
    m = w_ref.shape[1]
    return [functools.partial(_out_piece, y_ref, w_ref, x_ref, o_ref, r0, r1, c0, min(c0 + width, m))
            for c0 in range(0, m, width)]


def _rglru_gate_groups(y, wri_ref, bri_ref, lam_ref):
    w = y.shape[1]
    gw = wri_ref.shape[1]
    yb = y.astype(BF16)
    logsig = _log_sigmoid(lam_ref[...])
    for g in range(w // gw):
        cols = slice(g * gw, (g + 1) * gw)
        ri = jnp.dot(yb[:, cols], wri_ref[g], preferred_element_type=F32)
        r = _sigmoid(ri[:, :gw] + bri_ref[0:1, cols])
        i = _sigmoid(ri[:, gw:] + bri_ref[1:2, cols])
        log_a = C_A * r * logsig[:, cols]
        th = jnp.tanh(log_a)
        one_minus_a2 = -2.0 * th / (1.0 - th)
        yield jnp.exp(log_a), jnp.sqrt(one_minus_a2) * (i * y[:, cols])


def _rglru_gates(y, wri_ref, bri_ref, lam_ref):
    parts = list(_rglru_gate_groups(y, wri_ref, bri_ref, lam_ref))
    return jnp.concatenate([p[0] for p in parts], axis=1), jnp.concatenate([p[1] for p in parts], axis=1)


def _group_scan(a3, u3):
    t = lax.broadcasted_iota(jnp.int32, a3.shape, 1)
    s = 1
    while s < SUBLANES:
        keep = t >= s
        u3 = jnp.where(keep, a3 * pltpu.roll(u3, s, axis=1) + u3, u3)
        a3 = jnp.where(keep, a3 * pltpu.roll(a3, s, axis=1), a3)
        s *= 2
    return a3, u3


EVEN_TILE_STAGES = 8


def _even_rglru_branch(z_ref, y_ref, cw_ref, cb_ref, wri_ref, bri_ref, lam_ref, conv_ref, h_ref,
                       xbuf, hcar, hbuf, *, tile, w_a):
    xa = z_ref[:, 0:w_a]
    xbuf[SUBLANES:SUBLANES + tile, :] = xa
    cw = cw_ref[...]
    y = cb_ref[...] + cw[3:4] * xa
    for i in range(CONV_W - 1):
        off = SUBLANES - (CONV_W - 1) + i
        y = y + cw[i:i + 1] * xbuf[off:off + tile, :]
    xbuf[0:SUBLANES, :] = xbuf[tile:tile + SUBLANES, :]
    conv_ref[...] = xbuf[0:SUBLANES, :]
    yield

    a_parts, u_parts = [], []
    for a_g, u_g in _rglru_gate_groups(y, wri_ref, bri_ref, lam_ref):
        a_parts.append(a_g)
        u_parts.append(u_g)
        yield
    a = jnp.concatenate(a_parts, axis=1)
    u = jnp.concatenate(u_parts, axis=1)

    ng = tile // SUBLANES
    a3, u3 = _group_scan(a.reshape(ng, SUBLANES, w_a), u.reshape(ng, SUBLANES, w_a))
    yield
    carry = hcar[0:1, :]
    for g in range(ng):
        hg = u3[g] + a3[g] * carry
        hbuf[g * SUBLANES:(g + 1) * SUBLANES, :] = hg
        carry = hg[SUBLANES - 1:SUBLANES, :]
    hcar[0:1, :] = carry
    h_ref[...] = jnp.broadcast_to(carry, h_ref.shape)
    yield
    y_ref[:, 0:w_a] = (hbuf[...] * _gelu_tanh(z_ref[:, w_a:2 * w_a])).astype(BF16)
    yield


def _head_mean_sq(x, seg):
    x2 = x * x
    hi = x2.astype(BF16)
    lo = (x2 - hi.astype(F32)).astype(BF16)
    return (jnp.dot(hi, seg, preferred_element_type=F32)
            + jnp.dot(lo, seg, preferred_element_type=F32)) * (1.0 / HD_B)


def _even_attn_branch(z_ref, y_ref, is_first, qg_ref, kg_ref, seg_ref, bias_ref, sink_ref, kl_ref, vl_ref,
                      kbuf, vbuf, *, tile, w_a):
    nq = H_B * HD_B
    nkv = KV_B * HD_B
    k = z_ref[:, 2 * w_a + nq:2 * w_a + nq + nkv]
    kbuf[WINDOW:WINDOW + tile, :] = k * lax.rsqrt(_head_mean_sq(k, seg_ref[0:nkv, 0:nkv]) + EPS) * kg_ref[...]
    vbuf[WINDOW:WINDOW + tile, :] = z_ref[:, 2 * w_a + nq + nkv:2 * w_a + nq + 2 * nkv]
    yield

    chains = [(nb, kv) for nb in range(tile // WINDOW) for kv in range(KV_B)]
    rows = G_B * WINDOW
    q_all = z_ref[:, 2 * w_a:2 * w_a + nq]
    q_all = q_all * lax.rsqrt(_head_mean_sq(q_all, seg_ref[...]) + EPS) * qg_ref[...] * (HD_B ** -0.5)
    qs = jnp.concatenate([q_all[nb * WINDOW:(nb + 1) * WINDOW, (kv * G_B + g) * HD_B:(kv * G_B + g + 1) * HD_B]
                          for nb, kv in chains for g in range(G_B)], axis=0).astype(BF16)
    yield
    col = lax.broadcasted_iota(jnp.int32, (rows, 2 * WINDOW), 1)
    s_parts = []
    for c, (nb, kv) in enumerate(chains):
        kk = kbuf[nb * WINDOW:(nb + 2) * WINDOW, kv * HD_B:(kv + 1) * HD_B]
        s = _dot_nt(qs[c * rows:(c + 1) * rows], kk) + bias_ref[kv]
        if nb == 0 and is_first is not False:
            s = jnp.where(jnp.logical_and(is_first, col < WINDOW), NEG, s)
        s_parts.append(s)
        yield
    s = jnp.concatenate(s_parts, axis=0)
    sink = jnp.concatenate([sink_ref[kv] for _, kv in chains], axis=0)
    m = jnp.maximum(jnp.max(s, axis=-1, keepdims=True), sink)
    p = jnp.exp(s - jnp.concatenate([m] * (2 * WINDOW // LANES), axis=1)).astype(BF16)
    psum = jnp.dot(p, jnp.ones((2 * WINDOW, LANES), BF16), preferred_element_type=F32)
    inv = (1.0 / (psum + jnp.exp(sink - m)))[:, 0:HD_B]
    yield
    o = jnp.concatenate([_dot(p[c * rows:(c + 1) * rows],
                              vbuf[nb * WINDOW:(nb + 2) * WINDOW, kv * HD_B:(kv + 1) * HD_B])
                         for c, (nb, kv) in enumerate(chains)], axis=0) * inv
    y_ref[:, w_a:w_a + nq] = jnp.concatenate(
        [jnp.concatenate([o[(c * G_B + g) * WINDOW:(c * G_B + g + 1) * WINDOW, :]
                          for c, (cb_, _) in enumerate(chains) if cb_ == nb for g in range(G_B)], axis=1)
         for nb in range(tile // WINDOW)], axis=0).astype(BF16)

    kbuf[0:WINDOW, :] = kbuf[tile:tile + WINDOW, :]
    vbuf[0:WINDOW, :] = vbuf[tile:tile + WINDOW, :]
    kl_ref[...] = kbuf[0:WINDOW, :]
    vl_ref[...] = vbuf[0:WINDOW, :]
    yield


def _even_tile(z_ref, y_ref, is_first, side, cw_ref, cb_ref, wri_ref, bri_ref, lam_ref, qg_ref, kg_ref,
               seg_ref, bias_ref, sink_ref, conv_ref, h_ref, kl_ref, vl_ref, xbuf, hcar, hbuf, kbuf, vbuf, *, tile, w_a):
    branches = [
        _even_rglru_branch(z_ref, y_ref, cw_ref, cb_ref, wri_ref, bri_ref, lam_ref, conv_ref, h_ref,
                           xbuf, hcar, hbuf, tile=tile, w_a=w_a),
        _even_attn_branch(z_ref, y_ref, is_first, qg_ref, kg_ref, seg_ref, bias_ref, sink_ref, kl_ref, vl_ref,
                          kbuf, vbuf, tile=tile, w_a=w_a)]
    while branches:
        branches = [b for b in branches if next(b, StopIteration) is not StopIteration]
        side.tick()


def _even_prompt_body(x_ref, xn_ref, g_ref, win_ref, wout_ref, cw_ref, cb_ref, wri_ref, bri_ref, lam_ref,
                      qg_ref, kg_ref, seg_ref, bias_ref, sink_ref, o_ref, conv_ref, h_ref, kl_ref, vl_ref,
                      zb0, zb1, yb0, yb1, nbuf, xbuf, hcar, hbuf, kbuf, vbuf, *, tile, w_a):
    step = pl.program_id(0)
    nkv = KV_B * HD_B

    @pl.when(step == 0)
    def _():
        xbuf[0:SUBLANES, :] = jnp.zeros((SUBLANES, w_a), F32)
        hcar[...] = jnp.zeros_like(hcar)
        kbuf[0:WINDOW, :] = jnp.zeros((WINDOW, nkv), F32)
        vbuf[0:WINDOW, :] = jnp.zeros((WINDOW, nkv), F32)
        _SideWork(_proj_pieces(x_ref, 0, tile, g_ref, nbuf, win_ref, zb0, PROJ_PIECE)).flush()

    mix = functools.partial(_even_tile, cw_ref=cw_ref, cb_ref=cb_ref, wri_ref=wri_ref, bri_ref=bri_ref,
                            lam_ref=lam_ref, qg_ref=qg_ref, kg_ref=kg_ref, seg_ref=seg_ref, bias_ref=bias_ref,
                            sink_ref=sink_ref, conv_ref=conv_ref, h_ref=h_ref, kl_ref=kl_ref, vl_ref=vl_ref,
                            xbuf=xbuf, hcar=hcar, hbuf=hbuf, kbuf=kbuf, vbuf=vbuf, tile=tile, w_a=w_a)
    side = _SideWork(_proj_pieces(x_ref, tile, 2 * tile, g_ref, nbuf, win_ref, zb1, PROJ_PIECE),
                     EVEN_TILE_STAGES)
    mix(zb0, yb0, step == 0, side)
    side.flush()
    side = _SideWork(_proj_pieces(xn_ref, 0, tile, g_ref, nbuf, win_ref, zb0, PROJ_PIECE)
                     + _out_pieces(yb0, wout_ref, x_ref, o_ref, 0, tile, PROJ_PIECE), EVEN_TILE_STAGES)
    mix(zb1, yb1, False, side)
    side.flush()
    _SideWork(_out_pieces(yb1, wout_ref, x_ref, o_ref, tile, 2 * tile, PROJ_PIECE)).flush()


def _even_prompt(x, g, w_in, w_out, cw, cb, wri, bri, lam, qg, kg, seg, bias, sink):
    n, d = x.shape
    zin = w_in.shape[1]
    w_a = cw.shape[1]
    tile = min(MIX_TILE, n // 2)
    assert n % (2 * tile) == 0 and tile % WINDOW == 0
    nt = n // tile
    nq, nkv = H_B * HD_B, KV_B * HD_B
    small = (g, w_in, w_out, cw, cb, wri, bri, lam, qg, kg, seg, bias, sink)
    return pl.pallas_call(
        functools.partial(_even_prompt_body, tile=tile, w_a=w_a),
        grid=(nt // 2,),
        in_specs=[pl.BlockSpec((2 * tile, d), lambda i: (i, 0)),
                  pl.BlockSpec((tile, d), lambda i: (jnp.minimum(2 * i + 2, nt - 1), 0))]
                 + [_resident(a.shape) for a in small],
        out_specs=[pl.BlockSpec((2 * tile, d), lambda i: (i, 0)),
                   pl.BlockSpec((SUBLANES, w_a), lambda i: (0, 0)),
                   pl.BlockSpec((SUBLANES, w_a), lambda i: (0, 0)),
                   pl.BlockSpec((WINDOW, nkv), lambda i: (0, 0)),
                   pl.BlockSpec((WINDOW, nkv), lambda i: (0, 0))],
        out_shape=[jax.ShapeDtypeStruct((n, d), F32),
                   jax.ShapeDtypeStruct((SUBLANES, w_a), F32),
                   jax.ShapeDtypeStruct((SUBLANES, w_a), F32),
                   jax.ShapeDtypeStruct((WINDOW, nkv), F32),
                   jax.ShapeDtypeStruct((WINDOW, nkv), F32)],
        scratch_shapes=[pltpu.VMEM((tile, zin), F32), pltpu.VMEM((tile, zin), F32),
                        pltpu.VMEM((tile, w_a + nq), BF16), pltpu.VMEM((tile, w_a + nq), BF16),
                        pltpu.VMEM((tile, d), BF16),
                        pltpu.VMEM((tile + SUBLANES, w_a), F32), pltpu.VMEM((SUBLANES, w_a), F32),
                        pltpu.VMEM((tile, w_a), F32),
                        pltpu.VMEM((tile + WINDOW, nkv), F32), pltpu.VMEM((tile + WINDOW, nkv), F32)],
        compiler_params=_params(), name="even_prompt")(x, x, *small)


def _even_sample_body(z_ref, conv_ref, h0_ref, ck_ref, cv_ref, cw_ref, cb_ref, wri_ref, bri_ref, lam_ref,
                      qg_ref, kg_ref, bias_ref, sink_ref, y_ref, hs_ref, ko_ref, vo_ref, xc, *, sb, dl, w_a):
    nq = H_B * HD_B
    nkv = KV_B * HD_B
    rows = sb * dl
    nkeys = WINDOW + dl

    xa3 = z_ref[:, 0:w_a].reshape(sb, dl, w_a)
    xc[:, SUBLANES:SUBLANES + dl, :] = xa3
    xc[:, SUBLANES - (CONV_W - 1):SUBLANES, :] = conv_ref[...]
    cw = cw_ref[...]
    y3 = cb_ref[...] + cw[3:4] * xa3
    for i in range(CONV_W - 1):
        off = SUBLANES - (CONV_W - 1) + i
        y3 = y3 + cw[i:i + 1] * xc[:, off:off + dl, :]
    y = y3.reshape(rows, w_a)

    a, u = _rglru_gates(y, wri_ref, bri_ref, lam_ref)
    a3 = a.reshape(sb, dl, w_a)
    u3 = u.reshape(sb, dl, w_a)
    t = lax.broadcasted_iota(jnp.int32, a3.shape, 1)
    h0 = jnp.broadcast_to(h0_ref[...][:, None, :], a3.shape)
    u3 = jnp.where(t == 0, u3 + a3 * h0, u3)
    _, h3 = _group_scan(a3, u3)
    hs = h3.reshape(rows, w_a)
    hs_ref[...] = hs
    y_ref[:, 0:w_a] = (hs * _gelu_tanh(z_ref[:, w_a:2 * w_a])).astype(BF16)

    q3 = z_ref[:, 2 * w_a:2 * w_a + nq].reshape(sb, dl, nq)
    k3 = z_ref[:, 2 * w_a + nq:2 * w_a + nq + nkv].reshape(sb, dl, nkv)
    v3 = z_ref[:, 2 * w_a + nq + nkv:2 * w_a + nq + 2 * nkv].reshape(sb, dl, nkv)
    kparts = []
    for hh in range(KV_B):
        kh = k3[:, :, hh * HD_B:(hh + 1) * HD_B]
        kparts.append(kh * _rms_scale(kh))
    kn3 = jnp.concatenate(kparts, axis=2) * kg_ref[...]
    ko_ref[:, 0:WINDOW - dl, :] = ck_ref[:, dl:WINDOW, :]
    ko_ref[:, WINDOW - dl:WINDOW, :] = kn3
    vo_ref[:, 0:WINDOW - dl, :] = cv_ref[:, dl:WINDOW, :]
    vo_ref[:, WINDOW - dl:WINDOW, :] = v3

    outs = []
    for kv in range(KV_B):
        hs_ = slice(kv * HD_B, (kv + 1) * HD_B)
        qs = jnp.concatenate([q3[:, :, (kv * G_B + g) * HD_B:(kv * G_B + g + 1) * HD_B]
                              for g in range(G_B)], axis=1)
        qs = qs * _rms_scale(qs) * qg_ref[...] * (HD_B ** -0.5)
        kc = jnp.concatenate([ck_ref[:, :, hs_], kn3[:, :, hs_]], axis=1)
        vc = jnp.concatenate([cv_ref[:, :, hs_], v3[:, :, hs_]], axis=1)
        s = jnp.einsum('bqd,bkd->bqk', qs.astype(BF16), kc.astype(BF16),
                       preferred_element_type=F32) + bias_ref[kv]
        sink = sink_ref[kv]
        m = jnp.maximum(jnp.max(s, axis=-1, keepdims=True), sink)
        p = jnp.exp(s - m)
        den = jnp.sum(p, axis=-1, keepdims=True) + jnp.exp(sink - m)
        o = jnp.einsum('bqk,bkd->bqd', p.astype(BF16), vc.astype(BF16),
                       preferred_element_type=F32) / den
        outs += [o[:, g * dl:(g + 1) * dl, :] for g in range(G_B)]
    y_ref[:, w_a:w_a + nq] = jnp.concatenate(outs, axis=2).reshape(rows, nq).astype(BF16)


def _even_sample(z, conv, h0, ck, cv, cw, cb, wri, bri, lam, qg, kg, bias, sink, *, dl):
    n, zin = z.shape
    db = n // dl
    w_a = cw.shape[1]
    sb = min(EVEN_SB, db)
    assert db % sb == 0 and dl == SUBLANES
    nq, nkv = H_B * HD_B, KV_B * HD_B
    rows = sb * dl
    return pl.pallas_call(
        functools.partial(_even_sample_body, sb=sb, dl=dl, w_a=w_a),
        grid=(db // sb,),
        in_specs=[pl.BlockSpec((rows, zin), lambda i: (i, 0)),
                  pl.BlockSpec((sb, CONV_W - 1, w_a), lambda i: (i, 0, 0)),
                  pl.BlockSpec((sb, w_a), lambda i: (i, 0)),
                  pl.BlockSpec((sb, WINDOW, nkv), lambda i: (i, 0, 0)),
                  pl.BlockSpec((sb, WINDOW, nkv), lambda i: (i, 0, 0))]
                 + [_resident(a.shape) for a in (cw, cb, wri, bri, lam, qg, kg, bias, sink)],
        out_specs=[pl.BlockSpec((rows, w_a + nq), lambda i: (i, 0)),
                   pl.BlockSpec((rows, w_a), lambda i: (i, 0)),
                   pl.BlockSpec((sb, WINDOW, nkv), lambda i: (i, 0, 0)),
                   pl.BlockSpec((sb, WINDOW, nkv), lambda i: (i, 0, 0))],
        out_shape=[jax.ShapeDtypeStruct((n, w_a + nq), BF16),
                   jax.ShapeDtypeStruct((n, w_a), F32),
                   jax.ShapeDtypeStruct((db, WINDOW, nkv), F32),
                   jax.ShapeDtypeStruct((db, WINDOW, nkv), F32)],
        scratch_shapes=[pltpu.VMEM((sb, 2 * SUBLANES, w_a), F32)],
        compiler_params=_params(), name="even_sample")(
            z, conv, h0, ck, cv, cw, cb, wri, bri, lam, qg, kg, bias, sink)


def _retention_log_decay():
    return [float(np.log1p(-np.exp2(np.float32(-5.0 - h)))) for h in range(H_C)]


def _rotate(x, cosf, sinf, axis):
    return x * cosf + pltpu.roll(x, x.shape[axis] // 2, axis=axis) * sinf


ODD_TILE_STAGES = 2 * H_C + 2 * H_D + 2


def _odd_tile(z_ref, y_ref, cosf, sinf, side, dmask_ref, cw_ref, cb_ref, gb_ref, s_s, c_s, n_s, m_s, qkbuf, *,
              tile):
    wq = H_C * DK_C
    wv = H_C * DV_C
    o_kc, o_vc, o_gc = wq, 2 * wq, 2 * wq + wv
    o_qk = 2 * wq + 2 * wv
    wqk = 2 * H_D * DK_D
    o_vd = o_qk + wqk
    o_od = o_vd + H_D * DV_D
    o_gt = o_od + H_D * DV_D
    lg = _retention_log_decay()
    ti = lax.broadcasted_iota(jnp.int32, (tile, 1), 0).astype(F32)

    for h in range(H_C):
        q = _rotate(z_ref[:, h * DK_C:(h + 1) * DK_C], cosf, sinf, 1)
        k = _rotate(z_ref[:, o_kc + h * DK_C:o_kc + (h + 1) * DK_C], cosf, sinf, 1) * (DK_C ** -0.5)
        v = z_ref[:, o_vc + h * DV_C:o_vc + (h + 1) * DV_C]
        xi = jnp.exp((ti + 1.0) * lg[h])
        zeta = jnp.exp((tile - 1.0 - ti) * lg[h])
        sc = _dot_nt(q, k) * dmask_ref[h]
        o = _dot(sc, v) + _dot(q * xi, s_s[h])
        side.tick()
        s_s[h] = math.exp(tile * lg[h]) * s_s[h] + _dot_tn(k * zeta, v)
        gate = z_ref[:, o_gc + h * DV_C:o_gc + (h + 1) * DV_C]
        y_ref[:, h * DV_C:(h + 1) * DV_C] = (o * _rms_scale(o) * _silu(gate)).astype(BF16)
        side.tick()

    xqk = z_ref[:, o_qk:o_qk + wqk]
    qkbuf[SUBLANES:SUBLANES + tile, :] = xqk
    cw = cw_ref[...]
    qk = cb_ref[...] + cw[3:4] * xqk
    for i in range(CONV_W - 1):
        off = SUBLANES - (CONV_W - 1) + i
        qk = qk + cw[i:i + 1] * qkbuf[off:off + tile, :]
    qkbuf[0:SUBLANES, :] = qkbuf[tile:tile + SUBLANES, :]
    qk = _silu(qk)
    side.tick()

    gates = z_ref[:, o_gt:o_gt + LANES] + gb_ref[...]
    logf = _log_sigmoid(gates)
    row = lax.broadcasted_iota(jnp.int32, gates.shape, 0)
    bsum = logf
    s = 1
    while s < tile:
        bsum = bsum + jnp.where(row >= s, pltpu.roll(bsum, s, axis=0), 0.0)
        s *= 2
    bsum_t = bsum.T
    gates_t = gates.T
    ii = lax.broadcasted_iota(jnp.int32, (tile, tile), 0)
    jj = lax.broadcasted_iota(jnp.int32, (tile, tile), 1)
    causal = jj <= ii
    side.tick()
    for h in range(H_D):
        q = qk[:, h * DK_D:(h + 1) * DK_D]
        k = qk[:, H_D * DK_D + h * DK_D:H_D * DK_D + (h + 1) * DK_D] * (DK_D ** -0.5)
        v = z_ref[:, o_vd + h * DV_D:o_vd + (h + 1) * DV_D]
        b_col = bsum[:, H_D + h:H_D + h + 1]
        b_row = bsum_t[H_D + h:H_D + h + 1, :]
        i_col = gates[:, h:h + 1]
        i_row = gates_t[h:h + 1, :]
        m_prev = m_s[h:h + 1, 0:1]
        dlog = jnp.where(causal, b_col - b_row + i_row, NEG)
        init_log = b_col + m_prev
        m_t = jnp.maximum(init_log, jnp.max(dlog, axis=-1, keepdims=True))
        w = jnp.exp(dlog - m_t)
        a0 = jnp.exp(init_log - m_t)
        sc = _dot_nt(q, k) * w
        num = _dot(sc, v) + a0 * _dot(q, c_s[h])
        den = jnp.sum(sc, axis=-1, keepdims=True) + a0 * jnp.sum(q * n_s[h:h + 1, :], axis=-1, keepdims=True)
        den = jnp.maximum(jnp.abs(den), jnp.exp(-m_t))
        hd = num / den
        og = z_ref[:, o_od + h * DV_D:o_od + (h + 1) * DV_D]
        y_ref[:, wv + h * DV_D:wv + (h + 1) * DV_D] = (hd * _sigmoid(og)).astype(BF16)
        side.tick()
        b_end = b_col[tile - 1:tile, :]
        log_end_col = b_end - b_col + i_col
        m_new = jnp.maximum(b_end + m_prev, jnp.max(log_end_col, axis=0, keepdims=True))
        w_end = jnp.exp(log_end_col - m_new)
        a_end = jnp.exp(b_end + m_prev - m_new)
        kw = k * w_end
        c_s[h] = a_end * c_s[h] + _dot_tn(kw, v)
        n_s[h:h + 1, :] = a_end * n_s[h:h + 1, :] + jnp.sum(kw, axis=0, keepdims=True)
        m_s[h:h + 1, :] = jnp.broadcast_to(m_new, (1, LANES))
        side.tick()


def _odd_prompt_body(x_ref, xn_ref, g_ref, win_ref, wout_ref, cos_ref, sin_ref, dmask_ref, cw_ref, cb_ref,
                     gb_ref, o_ref, conv_ref, so_ref, co_ref, no_ref, mo_ref,
                     zb0, zb1, yb0, yb1, nbuf, s_s, c_s, n_s, m_s, qkbuf, *, tile):
    step = pl.program_id(0)

    @pl.when(step == 0)
    def _():
        s_s[...] = jnp.zeros_like(s_s)
        c_s[...] = jnp.zeros_like(c_s)
        n_s[...] = jnp.zeros_like(n_s)
        m_s[...] = jnp.zeros_like(m_s)
        qkbuf[0:SUBLANES, :] = jnp.zeros((SUBLANES, qkbuf.shape[1]), F32)
        _SideWork(_proj_pieces(x_ref, 0, tile, g_ref, nbuf, win_ref, zb0, 2 * PROJ_PIECE)).flush()

    mix = functools.partial(_odd_tile, dmask_ref=dmask_ref, cw_ref=cw_ref, cb_ref=cb_ref, gb_ref=gb_ref,
                            s_s=s_s, c_s=c_s, n_s=n_s, m_s=m_s, qkbuf=qkbuf, tile=tile)
    side = _SideWork(_proj_pieces(x_ref, tile, 2 * tile, g_ref, nbuf, win_ref, zb1, 2 * PROJ_PIECE),
                     ODD_TILE_STAGES)
    mix(zb0, yb0, cos_ref[0:tile, :], sin_ref[0:tile, :], side)
    side.flush()
    side = _SideWork(_proj_pieces(xn_ref, 0, tile, g_ref, nbuf, win_ref, zb0, 2 * PROJ_PIECE)
                     + _out_pieces(yb0, wout_ref, x_ref, o_ref, 0, tile, PROJ_PIECE), ODD_TILE_STAGES)
    mix(zb1, yb1, cos_ref[tile:2 * tile, :], sin_ref[tile:2 * tile, :], side)
    side.flush()
    _SideWork(_out_pieces(yb1, wout_ref, x_ref, o_ref, tile, 2 * tile, PROJ_PIECE)).flush()
    conv_ref[...] = qkbuf[0:SUBLANES, :]
    so_ref[...] = s_s[...]
    co_ref[...] = c_s[...]
    no_ref[...] = n_s[...]
    mo_ref[...] = m_s[...]


def _odd_prompt(x, g, w_in, w_out, cosf, sinf, dmask, cw, cb, gb):
    n, d = x.shape
    zin = w_in.shape[1]
    tile = dmask.shape[1]
    assert n % (2 * tile) == 0
    nt = n // tile
    wy = H_C * DV_C + H_D * DV_D
    wqk = 2 * H_D * DK_D
    small = (dmask, cw, cb, gb)
    return pl.pallas_call(
        functools.partial(_odd_prompt_body, tile=tile),
        grid=(nt // 2,),
        in_specs=[pl.BlockSpec((2 * tile, d), lambda i: (i, 0)),
                  pl.BlockSpec((tile, d), lambda i: (jnp.minimum(2 * i + 2, nt - 1), 0)),
                  _resident(g.shape), _resident(w_in.shape), _resident(w_out.shape),
                  pl.BlockSpec((2 * tile, DK_C), lambda i: (i, 0)),
                  pl.BlockSpec((2 * tile, DK_C), lambda i: (i, 0))]
                 + [_resident(a.shape) for a in small],
        out_specs=[pl.BlockSpec((2 * tile, d), lambda i: (i, 0)),
                   pl.BlockSpec((SUBLANES, wqk), lambda i: (0, 0)),
                   pl.BlockSpec((H_C, DK_C, DV_C), lambda i: (0, 0, 0)),
                   pl.BlockSpec((H_D, DK_D, DV_D), lambda i: (0, 0, 0)),
                   pl.BlockSpec((SUBLANES, DK_D), lambda i: (0, 0)),
                   pl.BlockSpec((SUBLANES, LANES), lambda i: (0, 0))],
        out_shape=[jax.ShapeDtypeStruct((n, d), F32),
                   jax.ShapeDtypeStruct((SUBLANES, wqk), F32),
                   jax.ShapeDtypeStruct((H_C, DK_C, DV_C), F32),
                   jax.ShapeDtypeStruct((H_D, DK_D, DV_D), F32),
                   jax.ShapeDtypeStruct((SUBLANES, DK_D), F32),
                   jax.ShapeDtypeStruct((SUBLANES, LANES), F32)],
        scratch_shapes=[pltpu.VMEM((tile, zin), F32), pltpu.VMEM((tile, zin), F32),
                        pltpu.VMEM((tile, wy), BF16), pltpu.VMEM((tile, wy), BF16),
                        pltpu.VMEM((tile, d), BF16),
                        pltpu.VMEM((H_C, DK_C, DV_C), F32), pltpu.VMEM((H_D, DK_D, DV_D), F32),
                        pltpu.VMEM((SUBLANES, DK_D), F32), pltpu.VMEM((SUBLANES, LANES), F32),
                        pltpu.VMEM((tile + SUBLANES, wqk), F32)],
        compiler_params=_params(), name="odd_prompt")(x, x, g, w_in, w_out, cosf, sinf, *small)


def _odd_sample_body(z_ref, cos_ref, sin_ref, s_ref, conv_ref, c_ref, n_ref, m_ref, cw_ref, cb_ref, gb_ref,
                     y_ref, so_ref, co_ref, no_ref, mo_ref, xc, *, sb, dl):
    wq = H_C * DK_C
    wv = H_C * DV_C
    o_kc, o_vc, o_gc = wq, 2 * wq, 2 * wq + wv
    o_qk = 2 * wq + 2 * wv
    wqk = 2 * H_D * DK_D
    o_vd = o_qk + wqk
    o_od = o_vd + H_D * DV_D
    o_gt = o_od + H_D * DV_D
    lg = _retention_log_decay()
    rows = sb * dl

    def z3(c0, width):
        return z_ref[:, c0:c0 + width].reshape(sb, dl, width)

    cosf = cos_ref[...][None]
    sinf = sin_ref[...][None]
    t1 = lax.broadcasted_iota(jnp.int32, (1, dl, 1), 1)
    tf = t1.astype(F32)

    for h in range(H_C):
        q = _rotate(z3(h * DK_C, DK_C), cosf, sinf, 2)
        k = _rotate(z3(o_kc + h * DK_C, DK_C), cosf, sinf, 2) * (DK_C ** -0.5)
        v = z3(o_vc + h * DV_C, DV_C)
        xi = jnp.exp((tf + 1.0) * lg[h])
        zeta = jnp.exp((dl - 1.0 - tf) * lg[h])
        s0 = s_ref[:, h]
        o = jnp.einsum('bqd,bde->bqe', (q * xi).astype(BF16), s0.astype(BF16), preferred_element_type=F32)
        for s in range(dl):
            ks = k if s == 0 else pltpu.roll(k, s, axis=1)
            vs = v if s == 0 else pltpu.roll(v, s, axis=1)
            coef = jnp.sum(q * ks, axis=-1, keepdims=True) * math.exp(s * lg[h])
            o = o + jnp.where(t1 >= s, coef, 0.0) * vs
        upd = jnp.einsum('btd,bte->bde', (k * zeta).astype(BF16), v.astype(BF16), preferred_element_type=F32)
        so_ref[:, h] = math.exp(dl * lg[h]) * s0 + upd
        gate = z3(o_gc + h * DV_C, DV_C)
        y_ref[:, h * DV_C:(h + 1) * DV_C] = (o * _rms_scale(o) * _silu(gate)).reshape(rows, DV_C).astype(BF16)

    xqk3 = z3(o_qk, wqk)
    xc[:, SUBLANES:SUBLANES + dl, :] = xqk3
    xc[:, SUBLANES - (CONV_W - 1):SUBLANES, :] = conv_ref[...]
    cw = cw_ref[...]
    qk = cb_ref[...] + cw[3:4] * xqk3
    for i in range(CONV_W - 1):
        off = SUBLANES - (CONV_W - 1) + i
        qk = qk + cw[i:i + 1] * xc[:, off:off + dl, :]
    qk = _silu(qk)

    gates = z3(o_gt, LANES) + gb_ref[...]
    logf = _log_sigmoid(gates)
    t = lax.broadcasted_iota(jnp.int32, gates.shape, 1)
    bsum = logf
    s = 1
    while s < dl:
        bsum = bsum + jnp.where(t >= s, pltpu.roll(bsum, s, axis=1), 0.0)
        s *= 2
    bsum = pltpu.roll(bsum, LANES - H_D, axis=2)
    m0 = m_ref[...]
    init_log = bsum + m0
    m_t = init_log
    dlogs = []
    for s in range(dl):
        if s == 0:
            d = gates
        else:
            d = bsum - pltpu.roll(bsum, s, axis=1) + pltpu.roll(gates, s, axis=1)
        d = jnp.where(t >= s, d, NEG)
        dlogs.append(d)
        m_t = jnp.maximum(m_t, d)
    a0 = jnp.exp(init_log - m_t)
    ws = [jnp.exp(d - m_t) for d in dlogs]
    inv_floor = jnp.exp(-m_t)
    b_end = bsum[:, dl - 1:dl, :]
    log_end = b_end - bsum + gates
    m_new = jnp.maximum(b_end + m0, jnp.max(log_end, axis=1, keepdims=True))
    w_end = jnp.exp(log_end - m_new)
    a_end = jnp.exp(b_end + m0 - m_new)
    mo_ref[...] = m_new
    for h in range(H_D):
        q = qk[:, :, h * DK_D:(h + 1) * DK_D]
        k = qk[:, :, H_D * DK_D + h * DK_D:H_D * DK_D + (h + 1) * DK_D] * (DK_D ** -0.5)
        v = z3(o_vd + h * DV_D, DV_D)
        c0 = c_ref[:, h]
        n0 = n_ref[:, h:h + 1, :]
        a0h = a0[:, :, h:h + 1]
        num = a0h * jnp.einsum('bqd,bde->bqe', q.astype(BF16), c0.astype(BF16), preferred_element_type=F32)
        den = a0h * jnp.sum(q * n0, axis=-1, keepdims=True)
        for s in range(dl):
            ks = k if s == 0 else pltpu.roll(k, s, axis=1)
            vs = v if s == 0 else pltpu.roll(v, s, axis=1)
            coef = jnp.sum(q * ks, axis=-1, keepdims=True) * ws[s][:, :, h:h + 1]
            num = num + coef * vs
            den = den + coef
        den = jnp.maximum(jnp.abs(den), inv_floor[:, :, h:h + 1])
        og = z3(o_od + h * DV_D, DV_D)
        y_ref[:, wv + h * DV_D:wv + (h + 1) * DV_D] = (
            (num / den) * _sigmoid(og)).reshape(rows, DV_D).astype(BF16)
        kw = k * w_end[:, :, h:h + 1]
        aeh = a_end[:, :, h:h + 1]
        upd = jnp.einsum('btd,bte->bde', kw.astype(BF16), v.astype(BF16), preferred_element_type=F32)
        co_ref[:, h] = aeh * c0 + upd
        no_ref[:, h:h + 1, :] = aeh * n0 + jnp.sum(kw, axis=1, keepdims=True)


def _odd_sample(z, cosf, sinf, s0, conv, c0, n0, m0, cw, cb, gb, *, dl):
    n, zin = z.shape
    db = n // dl
    sb = min(ODD_SB, db)
    assert db % sb == 0 and dl == SUBLANES
    rows = sb * dl
    wy = H_C * DV_C + H_D * DV_D
    wqk = 2 * H_D * DK_D
    st_spec = pl.BlockSpec((sb, H_C, DK_C, DV_C), lambda i: (i, 0, 0, 0))
    n_spec = pl.BlockSpec((sb, H_D, DK_D), lambda i: (i, 0, 0))
    m_spec = pl.BlockSpec((sb, 1, LANES), lambda i: (i, 0, 0))
    return pl.pallas_call(
        functools.partial(_odd_sample_body, sb=sb, dl=dl),
        grid=(db // sb,),
        in_specs=[pl.BlockSpec((rows, zin), lambda i: (i, 0)), _resident(cosf.shape), _resident(sinf.shape),
                  st_spec, pl.BlockSpec((sb, CONV_W - 1, wqk), lambda i: (i, 0, 0)), st_spec, n_spec, m_spec]
                 + [_resident(a.shape) for a in (cw, cb, gb)],
        out_specs=[pl.BlockSpec((rows, wy), lambda i: (i, 0)), st_spec, st_spec, n_spec, m_spec],
        out_shape=[jax.ShapeDtypeStruct((n, wy), BF16),
                   jax.ShapeDtypeStruct(s0.shape, F32), jax.ShapeDtypeStruct(c0.shape, F32),
                   jax.ShapeDtypeStruct(n0.shape, F32), jax.ShapeDtypeStruct(m0.shape, F32)],
        scratch_shapes=[pltpu.VMEM((sb, 2 * SUBLANES, wqk), F32)],
        compiler_params=_params(), name="odd_sample")(z, cosf, sinf, s0, conv, c0, n0, m0, cw, cb, gb)


def _t5_bucket(dist):
    n = np.maximum(dist, 0)
    max_exact = N_BUCKETS // 2
    large = max_exact + (np.log(np.maximum(n, max_exact) / max_exact)
                         / math.log(MAX_DIST / max_exact) * (N_BUCKETS - max_exact)).astype(np.int32)
    return np.where(n < max_exact, n, np.minimum(large, N_BUCKETS - 1)).astype(np.int32)


def _attn_bias(rel_bias, nq_rows, nkeys, p0):
    rel = np.arange(nq_rows)[:, None] + WINDOW - np.arange(nkeys)[None, :]
    kpos_ok = (p0 - WINDOW + np.arange(nkeys)) >= 0
    mask = (rel >= 0) & (rel < WINDOW) & kpos_ok[None, :]
    onehot = jnp.asarray(_t5_bucket(rel)[:, :, None] == np.arange(N_BUCKETS), F32)
    bias = jnp.einsum('qkb,bh->hqk', onehot, rel_bias.astype(F32), precision=lax.Precision.HIGHEST)
    bias = jnp.where(mask[None], bias, NEG)
    return bias.reshape(KV_B, G_B * nq_rows, nkeys)


def _sink_rows(sinks, nq_rows, width=1):
    col = jnp.repeat(sinks.astype(F32).reshape(KV_B, G_B), nq_rows, axis=1).reshape(KV_B, G_B * nq_rows, 1)
    return jnp.broadcast_to(col, (KV_B, G_B * nq_rows, width))


def _rope_tables(p0, n):
    half = DK_C // 2
    pos = p0 + jnp.arange(n, dtype=F32)
    inv = ROPE_BASE ** (-jnp.arange(half, dtype=F32) / half)
    ang = pos[:, None] * inv[None, :]
    cos, sin = jnp.cos(ang), jnp.sin(ang)
    return jnp.concatenate([cos, cos], axis=1), jnp.concatenate([-sin, sin], axis=1)


def _decay_mask(tile):
    t = np.arange(tile)
    diff = (t[:, None] - t[None, :]).astype(np.float32)
    lg = np.asarray(_retention_log_decay(), np.float32)
    return jnp.asarray(np.where(diff >= 0, np.exp(np.maximum(diff, 0)[None] * lg[:, None, None]), 0.0), F32)


def _block_diag_gates(w_r, w_i, group):
    nb, c, _ = w_r.shape
    eye = jnp.eye(group, dtype=w_r.dtype)

    def bd(w):
        w4 = w.reshape(nb // group, group, c, c)
        return jnp.einsum('gncd,nm->gncmd', w4, eye).reshape(nb // group, group * c, group * c)

    return jnp.concatenate([bd(w_r), bd(w_i)], axis=2).astype(BF16)


def kernel(x_prompt, x_sample, state_a_conv, state_a_h, cache_b_k, cache_b_v, state_c_S, state_d_conv, state_d_C, state_d_n, state_d_m, norm_g, ffn1_w_in, ffn1_w_out, ffn2_w_in, ffn2_w_out, even_w_in, even_w_out, a_conv_w, a_conv_b, a_w_r, a_b_r, a_w_i, a_b_i, a_lambda, b_qk_norm, b_sinks, rel_bias, odd_w_in, odd_w_out, d_conv_w, d_conv_b, d_gate_b):
    bp, lp, d = x_prompt.shape
    db, dl, _ = x_sample.shape
    assert bp == 1, "prompt group is a single sequence"
    depth = norm_g.shape[0]
    w_a = a_conv_w.shape[2]
    nkv = KV_B * HD_B
    wqk = 2 * H_D * DK_D
    tile = min(MIX_TILE, lp // 2)

    xp = x_prompt.reshape(lp, d)
    xs = x_sample.reshape(db * dl, d)

    bias_p = _attn_bias(rel_bias, WINDOW, 2 * WINDOW, WINDOW)
    bias_s = _attn_bias(rel_bias, dl, WINDOW + dl, PAST_LEN)
    cos_p, sin_p = _rope_tables(0.0, lp)
    cos_s, sin_s = _rope_tables(float(PAST_LEN), dl)
    dmask = _decay_mask(tile)

    st = {}
    for li in range(depth):
        j = li // 2
        w1i, w1o = ffn1_w_in[li].astype(BF16), ffn1_w_out[li].astype(BF16)
        xp = _ffn(xp, norm_g[li, 0], w1i, w1o)
        xs = _ffn(xs, norm_g[li, 0], w1i, w1o)
        if li % 2 == 0:
            w_in = even_w_in[j].astype(BF16)
            w_out = even_w_out[j].astype(BF16)
            zs = _norm_proj(xs, norm_g[li, 1], w_in)
            shared = (a_conv_w[j], a_conv_b[j].reshape(1, w_a),
                      _block_diag_gates(a_w_r[j], a_w_i[j], 4),
                      jnp.stack([a_b_r[j], a_b_i[j]]), a_lambda[j].reshape(1, w_a),
                      b_qk_norm[j, 0].reshape(1, HD_B), jnp.tile(b_qk_norm[j, 1], KV_B).reshape(1, nkv))
            seg = jnp.kron(jnp.eye(H_B, dtype=F32), jnp.ones((HD_B, HD_B), F32)).astype(BF16)
            xp, conv_p, h_p, k_p, v_p = _even_prompt(
                xp, norm_g[li, 1].reshape(1, d), w_in, w_out, *shared[:5],
                jnp.tile(b_qk_norm[j, 0], H_B).reshape(1, H_B * HD_B), shared[6], seg,
                bias_p, _sink_rows(b_sinks[j], WINDOW, LANES))
            ys, hs_all, k_s, v_s = _even_sample(
                zs, state_a_conv[j], state_a_h[j], cache_b_k[j].reshape(db, WINDOW, nkv),
                cache_b_v[j].reshape(db, WINDOW, nkv), *shared, bias_s, _sink_rows(b_sinks[j], dl), dl=dl)
            st.setdefault('a_conv', ([], []))
            st['a_conv'][0].append(conv_p[SUBLANES - (CONV_W - 1):].reshape(1, CONV_W - 1, w_a))
            st['a_conv'][1].append(zs.reshape(db, dl, -1)[:, dl - (CONV_W - 1):, :w_a])
            st.setdefault('a_h', ([], []))
            st['a_h'][0].append(h_p[0:1])
            st['a_h'][1].append(hs_all.reshape(db, dl, w_a)[:, dl - 1])
            st.setdefault('b_k', ([], []))
            st['b_k'][0].append(k_p.reshape(1, WINDOW, KV_B, HD_B))
            st['b_k'][1].append(k_s.reshape(db, WINDOW, KV_B, HD_B))
            st.setdefault('b_v', ([], []))
            st['b_v'][0].append(v_p.reshape(1, WINDOW, KV_B, HD_B))
            st['b_v'][1].append(v_s.reshape(db, WINDOW, KV_B, HD_B))
        else:
            zin = odd_w_in.shape[2]
            zpad = -(-zin // LANES) * LANES
            w_in = jnp.pad(odd_w_in[j], ((0, 0), (0, zpad - zin))).astype(BF16)
            w_out = odd_w_out[j].astype(BF16)
            zs = _norm_proj(xs, norm_g[li, 1], w_in)
            o_qk = 2 * H_C * DK_C + 2 * H_C * DV_C
            gb = jnp.pad(d_gate_b[j], (0, LANES - 2 * H_D)).reshape(1, LANES)
            shared = (d_conv_w[j], d_conv_b[j].reshape(1, wqk), gb)
            xp, dconv_p, s_p, c_p, n_p, m_p = _odd_prompt(xp, norm_g[li, 1].reshape(1, d), w_in, w_out,
                                                          cos_p, sin_p, dmask, *shared)
            m0 = jnp.pad(state_d_m[j], ((0, 0), (0, LANES - H_D))).reshape(db, 1, LANES)
            ys, s_s, c_s, n_s, m_s = _odd_sample(zs, cos_s, sin_s, state_c_S[j], state_d_conv[j],
                                                 state_d_C[j], state_d_n[j], m0, *shared, dl=dl)
            st.setdefault('c_S', ([], []))
            st['c_S'][0].append(s_p[None])
            st['c_S'][1].append(s_s)
            st.setdefault('d_conv', ([], []))
            st['d_conv'][0].append(dconv_p[SUBLANES - (CONV_W - 1):].reshape(1, CONV_W - 1, wqk))
            st['d_conv'][1].append(zs.reshape(db, dl, -1)[:, dl - (CONV_W - 1):, o_qk:o_qk + wqk])
            st.setdefault('d_C', ([], []))
            st['d_C'][0].append(c_p[None])
            st['d_C'][1].append(c_s)
            st.setdefault('d_n', ([], []))
            st['d_n'][0].append(n_p[None, :H_D])
            st['d_n'][1].append(n_s)
            st.setdefault('d_m', ([], []))
            st['d_m'][0].append(m_p[:H_D, 0].reshape(1, H_D))
            st['d_m'][1].append(m_s[:, 0, :H_D])
        xs = _out_proj(ys, w_out, xs)
        w2i, w2o = ffn2_w_in[li].astype(BF16), ffn2_w_out[li].astype(BF16)
        xp = _ffn(xp, norm_g[li, 2], w2i, w2o)
        xs = _ffn(xs, norm_g[li, 2], w2i, w2o)

    outs = [xp.reshape(1, lp, d), xs.reshape(db, dl, d)]
    for name in ('a_conv', 'a_h', 'b_k', 'b_v', 'c_S', 'd_conv', 'd_C', 'd_n', 'd_m'):
        outs.append(jnp.stack(st[name][0]))
        outs.append(jnp.stack(st[name][1]))
    return tuple(outs)
```

```python
import functools
import math

import jax
import jax.numpy as jnp
import numpy as np
from jax import lax
from jax.experimental import pallas as pl
from jax.experimental.pallas import tpu as pltpu

F32 = jnp.float32
BF16 = jnp.bfloat16

PAST_LEN = 16384
EPS = 1e-6
CONV_W = 4
C_A = 8.0
NB_A = 16
H_B, KV_B, HD_B = 8, 2, 64
G_B = H_B // KV_B
WINDOW = 128
N_BUCKETS = 32
MAX_DIST = 128
H_C, DK_C, DV_C = 4, 128, 256
H_D, DK_D, DV_D = 4, 128, 256
ROPE_BASE = 10000.0
NEG = -1e30

LANES = 128
SUBLANES = 8
VMEM_LIMIT = 56 * 1024 * 1024

ROW_TILE = 512
FFN_TILE = 1024
MIX_TILE = 256
FF_CHUNK = 256
EVEN_SB = 32
ODD_SB = 8
PROJ_PIECE = 256
EVEN_IN_CHUNK = 1408
ODD_IN_CHUNK = 896


def _params(n_grid_dims=1):
    return pltpu.CompilerParams(dimension_semantics=("arbitrary",) * n_grid_dims,
                                vmem_limit_bytes=VMEM_LIMIT)


def _resident(shape):
    nd = len(shape)
    return pl.BlockSpec(shape, lambda i, _nd=nd: (0,) * _nd, pipeline_mode=pl.Buffered(1))


def _dot(a, b):
    return jnp.dot(a.astype(BF16), b.astype(BF16), preferred_element_type=F32)


def _dot_nt(a, b):
    return lax.dot_general(a.astype(BF16), b.astype(BF16), (((1,), (1,)), ((), ())),
                           preferred_element_type=F32)


def _dot_tn(a, b):
    return lax.dot_general(a.astype(BF16), b.astype(BF16), (((0,), (0,)), ((), ())),
                           preferred_element_type=F32)


def _rms_scale(x):
    return lax.rsqrt(jnp.mean(x * x, axis=-1, keepdims=True) + EPS)


def _sigmoid(x):
    return 1.0 / (1.0 + jnp.exp(-x))


def _silu(x):
    return x * _sigmoid(x)


def _log_sigmoid(x):
    return jnp.minimum(x, 0.0) - jnp.log1p(jnp.exp(-jnp.abs(x)))


def _gelu_tanh(x):
    k = 2.0 * math.sqrt(2.0 / math.pi)
    return x / (1.0 + jnp.exp(x * (-k - (k * 0.044715) * (x * x))))


def _sqrt_nonneg(x):
    return jnp.where(x > 0.0, x * lax.rsqrt(x), 0.0)


def _ffn_body(x_ref, g_ref, wg_ref, wu_ref, wo_ref, o_ref, *, d_ff, chunk):
    x = x_ref[...]
    nb = (x * _rms_scale(x) * g_ref[...]).astype(BF16)
    acc = None
    for c0 in range(0, d_ff, chunk):
        gate = jnp.dot(nb, wg_ref[:, c0:c0 + chunk], preferred_element_type=F32)
        up = jnp.dot(nb, wu_ref[:, c0:c0 + chunk], preferred_element_type=F32)
        mid = (_silu(gate) * up).astype(BF16)
        part = jnp.dot(mid, wo_ref[c0:c0 + chunk, :], preferred_element_type=F32)
        acc = part if acc is None else acc + part
    o_ref[...] = x + 0.5 * acc


def _ffn(x, g, wg, wu, wo):
    n, d = x.shape
    d_ff = wo.shape[0]
    tm = min(FFN_TILE, n)
    assert n % tm == 0 and d_ff % FF_CHUNK == 0
    return pl.pallas_call(
        functools.partial(_ffn_body, d_ff=d_ff, chunk=FF_CHUNK),
        grid=(n // tm,),
        in_specs=[pl.BlockSpec((tm, d), lambda i: (i, 0)), _resident((1, d)),
                  _resident(wg.shape), _resident(wu.shape), _resident(wo.shape)],
        out_specs=pl.BlockSpec((tm, d), lambda i: (i, 0)),
        out_shape=jax.ShapeDtypeStruct((n, d), F32),
        compiler_params=_params(), name="ffn")(x, g.reshape(1, d), wg, wu, wo)


def _ffn_cast_body(x_ref, g_ref, wg_ref, wu_ref, wo_ref, o_ref, wgb_ref, wub_ref, wob_ref, nb_ref, acc_ref):
    j = pl.program_id(0)

    @pl.when(j == 0)
    def _():
        x = x_ref[...]
        nb_ref[...] = (x * _rms_scale(x) * g_ref[...]).astype(BF16)
        acc_ref[...] = jnp.zeros_like(acc_ref)

    wg = wg_ref[...].astype(BF16)
    wu = wu_ref[...].astype(BF16)
    wo = wo_ref[...].astype(BF16)
    wgb_ref[...] = wg
    wub_ref[...] = wu
    wob_ref[...] = wo
    nb = nb_ref[...]
    gate = jnp.dot(nb, wg, preferred_element_type=F32)
    up = jnp.dot(nb, wu, preferred_element_type=F32)
    acc_ref[...] += jnp.dot((_silu(gate) * up).astype(BF16), wo, preferred_element_type=F32)

    @pl.when(j == pl.num_programs(0) - 1)
    def _():
        o_ref[...] = x_ref[...] + 0.5 * acc_ref[...]


def _ffn_cast(x, g, w_in, w_out, li):
    n, d = x.shape
    d_ff = w_out.shape[1]
    nc = d_ff // FF_CHUNK
    assert d_ff % FF_CHUNK == 0
    return pl.pallas_call(
        _ffn_cast_body, grid=(nc,),
        in_specs=[pl.BlockSpec((n, d), lambda j: (0, 0)), pl.BlockSpec((1, d), lambda j: (0, 0)),
                  pl.BlockSpec((None, d, FF_CHUNK), lambda j: (li, 0, j)),
                  pl.BlockSpec((None, d, FF_CHUNK), lambda j: (li, 0, j + nc)),
                  pl.BlockSpec((None, FF_CHUNK, d), lambda j: (li, j, 0))],
        out_specs=[pl.BlockSpec((n, d), lambda j: (0, 0)),
                   pl.BlockSpec((d, FF_CHUNK), lambda j: (0, j)),
                   pl.BlockSpec((d, FF_CHUNK), lambda j: (0, j)),
                   pl.BlockSpec((FF_CHUNK, d), lambda j: (j, 0))],
        out_shape=[jax.ShapeDtypeStruct((n, d), F32), jax.ShapeDtypeStruct((d, d_ff), BF16),
                   jax.ShapeDtypeStruct((d, d_ff), BF16), jax.ShapeDtypeStruct((d_ff, d), BF16)],
        scratch_shapes=[pltpu.VMEM((n, d), BF16), pltpu.VMEM((n, d), F32)],
        compiler_params=_params(), name="ffn_cast")(x, g.reshape(1, d), w_in, w_in, w_out)


def _norm_proj_cast_body(x_ref, g_ref, w_ref, z_ref, wb_ref, nb_ref, *, chunk, valid):
    j = pl.program_id(0)

    @pl.when(j == 0)
    def _():
        x = x_ref[...]
        nb_ref[...] = (x * _rms_scale(x) * g_ref[...]).astype(BF16)

    col = j * chunk + lax.broadcasted_iota(jnp.int32, w_ref.shape, 1)
    w = jnp.where(col < valid, w_ref[...], 0.0).astype(BF16)
    wb_ref[...] = w
    z_ref[...] = jnp.dot(nb_ref[...], w, preferred_element_type=F32)


def _norm_proj_cast(x, g, w_all, li, chunk):
    n, d = x.shape
    m = w_all.shape[2]
    nc = -(-m // chunk)
    mp = nc * chunk
    return pl.pallas_call(
        functools.partial(_norm_proj_cast_body, chunk=chunk, valid=m), grid=(nc,),
        in_specs=[pl.BlockSpec((n, d), lambda j: (0, 0)), pl.BlockSpec((1, d), lambda j: (0, 0)),
                  pl.BlockSpec((None, d, chunk), lambda j: (li, 0, j))],
        out_specs=[pl.BlockSpec((n, chunk), lambda j: (0, j)), pl.BlockSpec((d, chunk), lambda j: (0, j))],
        out_shape=[jax.ShapeDtypeStruct((n, mp), F32), jax.ShapeDtypeStruct((d, mp), BF16)],
        scratch_shapes=[pltpu.VMEM((n, d), BF16)],
        compiler_params=_params(), name="norm_proj_cast")(x, g.reshape(1, d), w_all)


def _out_proj_cast_body(y_ref, w_ref, x_ref, o_ref, wb_ref):
    w = w_ref[...].astype(BF16)
    wb_ref[...] = w
    o_ref[...] = x_ref[...] + jnp.dot(y_ref[...], w, preferred_element_type=F32)


def _out_proj_cast(y, w_all, li, x):
    n, k = y.shape
    d = w_all.shape[2]
    chunk = PROJ_PIECE
    assert d % chunk == 0
    return pl.pallas_call(
        _out_proj_cast_body, grid=(d // chunk,),
        in_specs=[pl.BlockSpec((n, k), lambda j: (0, 0)),
                  pl.BlockSpec((None, k, chunk), lambda j: (li, 0, j)),
                  pl.BlockSpec((n, chunk), lambda j: (0, j))],
        out_specs=[pl.BlockSpec((n, chunk), lambda j: (0, j)), pl.BlockSpec((k, chunk), lambda j: (0, j))],
        out_shape=[jax.ShapeDtypeStruct((n, d), F32), jax.ShapeDtypeStruct((k, d), BF16)],
        compiler_params=_params(), name="out_proj_cast")(y, w_all, x)


class _SideWork:
    def __init__(self, pieces=(), stages=1):
        self.pieces, self.stages, self.done, self.stage = list(pieces), stages, 0, 0

    def tick(self):
        self.stage += 1
        upto = min(len(self.pieces), -(-len(self.pieces) * self.stage // self.stages))
        while self.done < upto:
            self.pieces[self.done]()
            self.done += 1

    def flush(self):
        self.stage = self.stages - 1
        self.tick()


def _norm_piece(x_ref, r0, r1, g_ref, nb_ref):
    x = x_ref[r0:r1, :]
    nb_ref[...] = (x * _rms_scale(x) * g_ref[...]).astype(BF16)


def _proj_piece(nb_ref, w_ref, z_ref, c0, c1):
    z_ref[:, c0:c1] = jnp.dot(nb_ref[...], w_ref[:, c0:c1], preferred_element_type=F32)


def _proj_pieces(x_ref, r0, r1, g_ref, nb_ref, w_ref, z_ref, width):
    m = w_ref.shape[1]
    return [functools.partial(_norm_piece, x_ref, r0, r1, g_ref, nb_ref)] + [
        functools.partial(_proj_piece, nb_ref, w_ref, z_ref, c0, min(c0 + width, m))
        for c0 in range(0, m, width)]


def _out_piece(y_ref, y_val, w_ref, x_ref, o_ref, r0, r1, c0, c1):
    if not y_val:
        y_val.append(y_ref[...])
    o_ref[r0:r1, c0:c1] = x_ref[r0:r1, c0:c1] + jnp.dot(y_val[0], w_ref[:, c0:c1],
                                                         preferred_element_type=F32)


def _out_pieces(y_ref, w_ref, x_ref, o_ref, r0, r1, width):
    m = w_ref.shape[1]
    y_val = []
    return [functools.partial(_out_piece, y_ref, y_val, w_ref, x_ref, o_ref, r0, r1, c0, min(c0 + width, m))
            for c0 in range(0, m, width)]


def _rglru_gate_groups(y, wri_ref, bri_ref, lam_ref):
    w = y.shape[1]
    gw = wri_ref.shape[1]
    yb = y.astype(BF16)
    logsig = _log_sigmoid(lam_ref[...])
    for g in range(w // gw):
        cols = slice(g * gw, (g + 1) * gw)
        ri = jnp.dot(yb[:, cols], wri_ref[g], preferred_element_type=F32)
        r = _sigmoid(ri[:, :gw] + bri_ref[0:1, cols])
        i = _sigmoid(ri[:, gw:] + bri_ref[1:2, cols])
        log_a = C_A * r * logsig[:, cols]
        th = jnp.tanh(log_a)
        one_minus_a2 = -2.0 * th / (1.0 - th)
        yield jnp.exp(log_a), _sqrt_nonneg(one_minus_a2) * (i * y[:, cols])


def _rglru_gates(y, wri_ref, bri_ref, lam_ref):
    parts = list(_rglru_gate_groups(y, wri_ref, bri_ref, lam_ref))
    return jnp.concatenate([p[0] for p in parts], axis=1), jnp.concatenate([p[1] for p in parts], axis=1)


def _group_scan(a3, u3):
    t = lax.broadcasted_iota(jnp.int32, a3.shape, 1)
    s = 1
    while s < SUBLANES:
        keep = t >= s
        u3 = jnp.where(keep, a3 * pltpu.roll(u3, s, axis=1) + u3, u3)
        a3 = jnp.where(keep, a3 * pltpu.roll(a3, s, axis=1), a3)
        s *= 2
    return a3, u3


EVEN_TILE_STAGES = 8


def _even_rglru_branch(z_ref, y_ref, cw_ref, cb_ref, wri_ref, bri_ref, lam_ref, conv_ref, h_ref,
                       xbuf, hcar, hbuf, *, tile, w_a):
    xa = z_ref[:, 0:w_a]
    xbuf[SUBLANES:SUBLANES + tile, :] = xa
    cw = cw_ref[...]
    y = cb_ref[...] + cw[3:4] * xa
    for i in range(CONV_W - 1):
        off = SUBLANES - (CONV_W - 1) + i
        y = y + cw[i:i + 1] * xbuf[off:off + tile, :]
    xbuf[0:SUBLANES, :] = xbuf[tile:tile + SUBLANES, :]
    conv_ref[...] = xbuf[0:SUBLANES, :]
    yield

    a_parts, u_parts = [], []
    for a_g, u_g in _rglru_gate_groups(y, wri_ref, bri_ref, lam_ref):
        a_parts.append(a_g)
        u_parts.append(u_g)
        yield
    a = jnp.concatenate(a_parts, axis=1)
    u = jnp.concatenate(u_parts, axis=1)

    ng = tile // SUBLANES
    a3, u3 = _group_scan(a.reshape(ng, SUBLANES, w_a), u.reshape(ng, SUBLANES, w_a))
    yield
    carry = hcar[0:1, :]
    for g in range(ng):
        hg = u3[g] + a3[g] * carry
        hbuf[g * SUBLANES:(g + 1) * SUBLANES, :] = hg
        carry = hg[SUBLANES - 1:SUBLANES, :]
    hcar[0:1, :] = carry
    h_ref[...] = jnp.broadcast_to(carry, h_ref.shape)
    yield
    y_ref[:, 0:w_a] = (hbuf[...] * _gelu_tanh(z_ref[:, w_a:2 * w_a])).astype(BF16)
    yield


def _head_mean_sq(x, seg):
    x2 = x * x
    hi = x2.astype(BF16)
    lo = (x2 - hi.astype(F32)).astype(BF16)
    return (jnp.dot(hi, seg, preferred_element_type=F32)
            + jnp.dot(lo, seg, preferred_element_type=F32)) * (1.0 / HD_B)


def _even_attn_branch(z_ref, y_ref, is_first, qg_ref, kg_ref, seg_ref, bias_ref, sink_ref, kl_ref, vl_ref,
                      kbuf, vbuf, *, tile, w_a):
    nq = H_B * HD_B
    nkv = KV_B * HD_B
    k = z_ref[:, 2 * w_a + nq:2 * w_a + nq + nkv]
    kbuf[WINDOW:WINDOW + tile, :] = k * lax.rsqrt(_head_mean_sq(k, seg_ref[0:nkv, 0:nkv]) + EPS) * kg_ref[...]
    vbuf[WINDOW:WINDOW + tile, :] = z_ref[:, 2 * w_a + nq + nkv:2 * w_a + nq + 2 * nkv]
    yield

    chains = [(nb, kv) for nb in range(tile // WINDOW) for kv in range(KV_B)]
    rows = G_B * WINDOW
    q_all = z_ref[:, 2 * w_a:2 * w_a + nq]
    q_all = q_all * lax.rsqrt(_head_mean_sq(q_all, seg_ref[...]) + EPS) * qg_ref[...] * (HD_B ** -0.5)
    qs = jnp.concatenate([q_all[nb * WINDOW:(nb + 1) * WINDOW, (kv * G_B + g) * HD_B:(kv * G_B + g + 1) * HD_B]
                          for nb, kv in chains for g in range(G_B)], axis=0).astype(BF16)
    yield
    col = lax.broadcasted_iota(jnp.int32, (rows, 2 * WINDOW), 1)
    s_parts = []
    for c, (nb, kv) in enumerate(chains):
        kk = kbuf[nb * WINDOW:(nb + 2) * WINDOW, kv * HD_B:(kv + 1) * HD_B]
        s = _dot_nt(qs[c * rows:(c + 1) * rows], kk) + bias_ref[kv]
        if nb == 0 and is_first is not False:
            s = jnp.where(jnp.logical_and(is_first, col < WINDOW), NEG, s)
        s_parts.append(s)
        yield
    s = jnp.concatenate(s_parts, axis=0)
    sink = jnp.concatenate([sink_ref[kv] for _, kv in chains], axis=0)
    m = jnp.maximum(jnp.max(s, axis=-1, keepdims=True), sink)
    p = jnp.exp(s - jnp.concatenate([m] * (2 * WINDOW // LANES), axis=1)).astype(BF16)
    psum = jnp.dot(p, jnp.ones((2 * WINDOW, LANES), BF16), preferred_element_type=F32)
    inv = (1.0 / (psum + jnp.exp(sink - m)))[:, 0:HD_B]
    yield
    o = jnp.concatenate([_dot(p[c * rows:(c + 1) * rows],
                              vbuf[nb * WINDOW:(nb + 2) * WINDOW, kv * HD_B:(kv + 1) * HD_B])
                         for c, (nb, kv) in enumerate(chains)], axis=0) * inv
    y_ref[:, w_a:w_a + nq] = jnp.concatenate(
        [jnp.concatenate([o[(c * G_B + g) * WINDOW:(c * G_B + g + 1) * WINDOW, :]
                          for c, (cb_, _) in enumerate(chains) if cb_ == nb for g in range(G_B)], axis=1)
         for nb in range(tile // WINDOW)], axis=0).astype(BF16)

    kbuf[0:WINDOW, :] = kbuf[tile:tile + WINDOW, :]
    vbuf[0:WINDOW, :] = vbuf[tile:tile + WINDOW, :]
    kl_ref[...] = kbuf[0:WINDOW, :]
    vl_ref[...] = vbuf[0:WINDOW, :]
    yield


def _even_tile(z_ref, y_ref, is_first, side, cw_ref, cb_ref, wri_ref, bri_ref, lam_ref, qg_ref, kg_ref,
               seg_ref, bias_ref, sink_ref, conv_ref, h_ref, kl_ref, vl_ref, xbuf, hcar, hbuf, kbuf, vbuf, *, tile, w_a):
    branches = [
        _even_rglru_branch(z_ref, y_ref, cw_ref, cb_ref, wri_ref, bri_ref, lam_ref, conv_ref, h_ref,
                           xbuf, hcar, hbuf, tile=tile, w_a=w_a),
        _even_attn_branch(z_ref, y_ref, is_first, qg_ref, kg_ref, seg_ref, bias_ref, sink_ref, kl_ref, vl_ref,
                          kbuf, vbuf, tile=tile, w_a=w_a)]
    while branches:
        branches = [b for b in branches if next(b, StopIteration) is not StopIteration]
        side.tick()


def _even_prompt_body(x_ref, xn_ref, g_ref, win_ref, wout_ref, cw_ref, cb_ref, wri_ref, bri_ref, lam_ref,
                      qg_ref, kg_ref, seg_ref, bias_ref, sink_ref, o_ref, conv_ref, h_ref, kl_ref, vl_ref,
                      zb0, zb1, yb0, yb1, nbuf, xbuf, hcar, hbuf, kbuf, vbuf, *, tile, w_a):
    step = pl.program_id(0)
    nkv = KV_B * HD_B

    @pl.when(step == 0)
    def _():
        xbuf[0:SUBLANES, :] = jnp.zeros((SUBLANES, w_a), F32)
        hcar[...] = jnp.zeros_like(hcar)
        kbuf[0:WINDOW, :] = jnp.zeros((WINDOW, nkv), F32)
        vbuf[0:WINDOW, :] = jnp.zeros((WINDOW, nkv), F32)
        _SideWork(_proj_pieces(x_ref, 0, tile, g_ref, nbuf, win_ref, zb0, PROJ_PIECE)).flush()

    mix = functools.partial(_even_tile, cw_ref=cw_ref, cb_ref=cb_ref, wri_ref=wri_ref, bri_ref=bri_ref,
                            lam_ref=lam_ref, qg_ref=qg_ref, kg_ref=kg_ref, seg_ref=seg_ref, bias_ref=bias_ref,
                            sink_ref=sink_ref, conv_ref=conv_ref, h_ref=h_ref, kl_ref=kl_ref, vl_ref=vl_ref,
                            xbuf=xbuf, hcar=hcar, hbuf=hbuf, kbuf=kbuf, vbuf=vbuf, tile=tile, w_a=w_a)
    side = _SideWork(_proj_pieces(x_ref, tile, 2 * tile, g_ref, nbuf, win_ref, zb1, PROJ_PIECE),
                     EVEN_TILE_STAGES)
    mix(zb0, yb0, step == 0, side)
    side.flush()
    side = _SideWork(_proj_pieces(xn_ref, 0, tile, g_ref, nbuf, win_ref, zb0, PROJ_PIECE)
                     + _out_pieces(yb0, wout_ref, x_ref, o_ref, 0, tile, PROJ_PIECE), EVEN_TILE_STAGES)
    mix(zb1, yb1, False, side)
    side.flush()
    _SideWork(_out_pieces(yb1, wout_ref, x_ref, o_ref, tile, 2 * tile, PROJ_PIECE)).flush()


def _even_prompt(x, g, w_in, w_out, cw, cb, wri, bri, lam, qg, kg, seg, bias, sink):
    n, d = x.shape
    zin = w_in.shape[1]
    w_a = cw.shape[1]
    tile = min(MIX_TILE, n // 2)
    assert n % (2 * tile) == 0 and tile % WINDOW == 0
    nt = n // tile
    nq, nkv = H_B * HD_B, KV_B * HD_B
    small = (g, w_in, w_out, cw, cb, wri, bri, lam, qg, kg, seg, bias, sink)
    return pl.pallas_call(
        functools.partial(_even_prompt_body, tile=tile, w_a=w_a),
        grid=(nt // 2,),
        in_specs=[pl.BlockSpec((2 * tile, d), lambda i: (i, 0)),
                  pl.BlockSpec((tile, d), lambda i: (jnp.minimum(2 * i + 2, nt - 1), 0))]
                 + [_resident(a.shape) for a in small],
        out_specs=[pl.BlockSpec((2 * tile, d), lambda i: (i, 0)),
                   pl.BlockSpec((SUBLANES, w_a), lambda i: (0, 0)),
                   pl.BlockSpec((SUBLANES, w_a), lambda i: (0, 0)),
                   pl.BlockSpec((WINDOW, nkv), lambda i: (0, 0)),
                   pl.BlockSpec((WINDOW, nkv), lambda i: (0, 0))],
        out_shape=[jax.ShapeDtypeStruct((n, d), F32),
                   jax.ShapeDtypeStruct((SUBLANES, w_a), F32),
                   jax.ShapeDtypeStruct((SUBLANES, w_a), F32),
                   jax.ShapeDtypeStruct((WINDOW, nkv), F32),
                   jax.ShapeDtypeStruct((WINDOW, nkv), F32)],
        scratch_shapes=[pltpu.VMEM((tile, zin), F32), pltpu.VMEM((tile, zin), F32),
                        pltpu.VMEM((tile, w_a + nq), BF16), pltpu.VMEM((tile, w_a + nq), BF16),
                        pltpu.VMEM((tile, d), BF16),
                        pltpu.VMEM((tile + SUBLANES, w_a), F32), pltpu.VMEM((SUBLANES, w_a), F32),
                        pltpu.VMEM((tile, w_a), F32),
                        pltpu.VMEM((tile + WINDOW, nkv), F32), pltpu.VMEM((tile + WINDOW, nkv), F32)],
        compiler_params=_params(), name="even_prompt")(x, x, *small)


def _even_sample_body(z_ref, conv_ref, h0_ref, ck_ref, cv_ref, cw_ref, cb_ref, wri_ref, bri_ref, lam_ref,
                      qg_ref, kg_ref, bias_ref, sink_ref, y_ref, hs_ref, ko_ref, vo_ref, xc, *, sb, dl, w_a):
    nq = H_B * HD_B
    nkv = KV_B * HD_B
    rows = sb * dl
    nkeys = WINDOW + dl

    xa3 = z_ref[:, 0:w_a].reshape(sb, dl, w_a)
    xc[:, SUBLANES:SUBLANES + dl, :] = xa3
    xc[:, SUBLANES - (CONV_W - 1):SUBLANES, :] = conv_ref[...]
    cw = cw_ref[...]
    y3 = cb_ref[...] + cw[3:4] * xa3
    for i in range(CONV_W - 1):
        off = SUBLANES - (CONV_W - 1) + i
        y3 = y3 + cw[i:i + 1] * xc[:, off:off + dl, :]
    y = y3.reshape(rows, w_a)

    a, u = _rglru_gates(y, wri_ref, bri_ref, lam_ref)
    a3 = a.reshape(sb, dl, w_a)
    u3 = u.reshape(sb, dl, w_a)
    t = lax.broadcasted_iota(jnp.int32, a3.shape, 1)
    h0 = jnp.broadcast_to(h0_ref[...][:, None, :], a3.shape)
    u3 = jnp.where(t == 0, u3 + a3 * h0, u3)
    _, h3 = _group_scan(a3, u3)
    hs = h3.reshape(rows, w_a)
    hs_ref[...] = hs
    y_ref[:, 0:w_a] = (hs * _gelu_tanh(z_ref[:, w_a:2 * w_a])).astype(BF16)

    q3 = z_ref[:, 2 * w_a:2 * w_a + nq].reshape(sb, dl, nq)
    k3 = z_ref[:, 2 * w_a + nq:2 * w_a + nq + nkv].reshape(sb, dl, nkv)
    v3 = z_ref[:, 2 * w_a + nq + nkv:2 * w_a + nq + 2 * nkv].reshape(sb, dl, nkv)
    kparts = []
    for hh in range(KV_B):
        kh = k3[:, :, hh * HD_B:(hh + 1) * HD_B]
        kparts.append(kh * _rms_scale(kh))
    kn3 = jnp.concatenate(kparts, axis=2) * kg_ref[...]
    ko_ref[:, 0:WINDOW - dl, :] = ck_ref[:, dl:WINDOW, :]
    ko_ref[:, WINDOW - dl:WINDOW, :] = kn3
    vo_ref[:, 0:WINDOW - dl, :] = cv_ref[:, dl:WINDOW, :]
    vo_ref[:, WINDOW - dl:WINDOW, :] = v3

    outs = []
    for kv in range(KV_B):
        hs_ = slice(kv * HD_B, (kv + 1) * HD_B)
        qs = jnp.concatenate([q3[:, :, (kv * G_B + g) * HD_B:(kv * G_B + g + 1) * HD_B]
                              for g in range(G_B)], axis=1)
        qs = qs * _rms_scale(qs) * qg_ref[...] * (HD_B ** -0.5)
        kc = jnp.concatenate([ck_ref[:, :, hs_], kn3[:, :, hs_]], axis=1)
        vc = jnp.concatenate([cv_ref[:, :, hs_], v3[:, :, hs_]], axis=1)
        s = jnp.einsum('bqd,bkd->bqk', qs.astype(BF16), kc.astype(BF16),
                       preferred_element_type=F32) + bias_ref[kv]
        sink = sink_ref[kv]
        m = jnp.maximum(jnp.max(s, axis=-1, keepdims=True), sink)
        p = jnp.exp(s - m)
        den = jnp.sum(p, axis=-1, keepdims=True) + jnp.exp(sink - m)
        o = jnp.einsum('bqk,bkd->bqd', p.astype(BF16), vc.astype(BF16),
                       preferred_element_type=F32) / den
        outs += [o[:, g * dl:(g + 1) * dl, :] for g in range(G_B)]
    y_ref[:, w_a:w_a + nq] = jnp.concatenate(outs, axis=2).reshape(rows, nq).astype(BF16)


def _even_sample(z, conv, h0, ck, cv, cw, cb, wri, bri, lam, qg, kg, bias, sink, *, dl):
    n, zin = z.shape
    db = n // dl
    w_a = cw.shape[1]
    sb = min(EVEN_SB, db)
    assert db % sb == 0 and dl == SUBLANES
    nq, nkv = H_B * HD_B, KV_B * HD_B
    rows = sb * dl
    return pl.pallas_call(
        functools.partial(_even_sample_body, sb=sb, dl=dl, w_a=w_a),
        grid=(db // sb,),
        in_specs=[pl.BlockSpec((rows, zin), lambda i: (i, 0)),
                  pl.BlockSpec((sb, CONV_W - 1, w_a), lambda i: (i, 0, 0)),
                  pl.BlockSpec((sb, w_a), lambda i: (i, 0)),
                  pl.BlockSpec((sb, WINDOW, nkv), lambda i: (i, 0, 0)),
                  pl.BlockSpec((sb, WINDOW, nkv), lambda i: (i, 0, 0))]
                 + [_resident(a.shape) for a in (cw, cb, wri, bri, lam, qg, kg, bias, sink)],
        out_specs=[pl.BlockSpec((rows, w_a + nq), lambda i: (i, 0)),
                   pl.BlockSpec((rows, w_a), lambda i: (i, 0)),
                   pl.BlockSpec((sb, WINDOW, nkv), lambda i: (i, 0, 0)),
                   pl.BlockSpec((sb, WINDOW, nkv), lambda i: (i, 0, 0))],
        out_shape=[jax.ShapeDtypeStruct((n, w_a + nq), BF16),
                   jax.ShapeDtypeStruct((n, w_a), F32),
                   jax.ShapeDtypeStruct((db, WINDOW, nkv), F32),
                   jax.ShapeDtypeStruct((db, WINDOW, nkv), F32)],
        scratch_shapes=[pltpu.VMEM((sb, 2 * SUBLANES, w_a), F32)],
        compiler_params=_params(), name="even_sample")(
            z, conv, h0, ck, cv, cw, cb, wri, bri, lam, qg, kg, bias, sink)


def _retention_log_decay():
    return [float(np.log1p(-np.exp2(np.float32(-5.0 - h)))) for h in range(H_C)]


def _rotate(x, cosf, sinf, axis):
    return x * cosf + pltpu.roll(x, x.shape[axis] // 2, axis=axis) * sinf


ODD_TILE_STAGES = 2 * H_C + 2 * H_D + 2


def _odd_tile(z_ref, y_ref, cosf, sinf, side, dmask_ref, cw_ref, cb_ref, gb_ref, s_s, c_s, n_s, m_s, qkbuf, *,
              tile):
    wq = H_C * DK_C
    wv = H_C * DV_C
    o_kc, o_vc, o_gc = wq, 2 * wq, 2 * wq + wv
    o_qk = 2 * wq + 2 * wv
    wqk = 2 * H_D * DK_D
    o_vd = o_qk + wqk
    o_od = o_vd + H_D * DV_D
    o_gt = o_od + H_D * DV_D
    lg = _retention_log_decay()
    ti = lax.broadcasted_iota(jnp.int32, (tile, 1), 0).astype(F32)

    for h in range(H_C):
        q = _rotate(z_ref[:, h * DK_C:(h + 1) * DK_C], cosf, sinf, 1)
        k = _rotate(z_ref[:, o_kc + h * DK_C:o_kc + (h + 1) * DK_C], cosf, sinf, 1) * (DK_C ** -0.5)
        v = z_ref[:, o_vc + h * DV_C:o_vc + (h + 1) * DV_C]
        xi = jnp.exp((ti + 1.0) * lg[h])
        zeta = jnp.exp((tile - 1.0 - ti) * lg[h])
        sc = _dot_nt(q, k) * dmask_ref[h]
        o = _dot(sc, v) + _dot(q * xi, s_s[h])
        side.tick()
        s_s[h] = math.exp(tile * lg[h]) * s_s[h] + _dot_tn(k * zeta, v)
        gate = z_ref[:, o_gc + h * DV_C:o_gc + (h + 1) * DV_C]
        y_ref[:, h * DV_C:(h + 1) * DV_C] = (o * _rms_scale(o) * _silu(gate)).astype(BF16)
        side.tick()

    xqk = z_ref[:, o_qk:o_qk + wqk]
    qkbuf[SUBLANES:SUBLANES + tile, :] = xqk
    cw = cw_ref[...]
    qk = cb_ref[...] + cw[3:4] * xqk
    for i in range(CONV_W - 1):
        off = SUBLANES - (CONV_W - 1) + i
        qk = qk + cw[i:i + 1] * qkbuf[off:off + tile, :]
    qkbuf[0:SUBLANES, :] = qkbuf[tile:tile + SUBLANES, :]
    qk = _silu(qk)
    side.tick()

    gates = z_ref[:, o_gt:o_gt + LANES] + gb_ref[...]
    logf = _log_sigmoid(gates)
    row = lax.broadcasted_iota(jnp.int32, gates.shape, 0)
    bsum = logf
    s = 1
    while s < tile:
        bsum = bsum + jnp.where(row >= s, pltpu.roll(bsum, s, axis=0), 0.0)
        s *= 2
    bsum_t = bsum.T
    gates_t = gates.T
    ii = lax.broadcasted_iota(jnp.int32, (tile, tile), 0)
    jj = lax.broadcasted_iota(jnp.int32, (tile, tile), 1)
    causal = jj <= ii
    side.tick()
    for h in range(H_D):
        q = qk[:, h * DK_D:(h + 1) * DK_D]
        k = qk[:, H_D * DK_D + h * DK_D:H_D * DK_D + (h + 1) * DK_D] * (DK_D ** -0.5)
        v = z_ref[:, o_vd + h * DV_D:o_vd + (h + 1) * DV_D]
        b_col = bsum[:, H_D + h:H_D + h + 1]
        b_row = bsum_t[H_D + h:H_D + h + 1, :]
        i_col = gates[:, h:h + 1]
        i_row = gates_t[h:h + 1, :]
        m_prev = m_s[h:h + 1, 0:1]
        dlog = jnp.where(causal, b_col - b_row + i_row, NEG)
        init_log = b_col + m_prev
        m_t = jnp.maximum(init_log, jnp.max(dlog, axis=-1, keepdims=True))
        w = jnp.exp(dlog - m_t)
        a0 = jnp.exp(init_log - m_t)
        sc = _dot_nt(q, k) * w
        num = _dot(sc, v) + a0 * _dot(q, c_s[h])
        den = jnp.sum(sc, axis=-1, keepdims=True) + a0 * jnp.sum(q * n_s[h:h + 1, :], axis=-1, keepdims=True)
        den = jnp.maximum(jnp.abs(den), jnp.exp(-m_t))
        hd = num / den
        og = z_ref[:, o_od + h * DV_D:o_od + (h + 1) * DV_D]
        y_ref[:, wv + h * DV_D:wv + (h + 1) * DV_D] = (hd * _sigmoid(og)).astype(BF16)
        side.tick()
        b_end = b_col[tile - 1:tile, :]
        log_end_col = b_end - b_col + i_col
        m_new = jnp.maximum(b_end + m_prev, jnp.max(log_end_col, axis=0, keepdims=True))
        w_end = jnp.exp(log_end_col - m_new)
        a_end = jnp.exp(b_end + m_prev - m_new)
        kw = k * w_end
        c_s[h] = a_end * c_s[h] + _dot_tn(kw, v)
        n_s[h:h + 1, :] = a_end * n_s[h:h + 1, :] + jnp.sum(kw, axis=0, keepdims=True)
        m_s[h:h + 1, :] = jnp.broadcast_to(m_new, (1, LANES))
        side.tick()


def _odd_prompt_body(x_ref, xn_ref, g_ref, win_ref, wout_ref, rbase_ref, roff_ref, dmask_ref, cw_ref, cb_ref,
                     gb_ref, o_ref, conv_ref, so_ref, co_ref, no_ref, mo_ref,
                     zb0, zb1, yb0, yb1, nbuf, s_s, c_s, n_s, m_s, qkbuf, *, tile):
    step = pl.program_id(0)

    @pl.when(step == 0)
    def _():
        s_s[...] = jnp.zeros_like(s_s)
        c_s[...] = jnp.zeros_like(c_s)
        n_s[...] = jnp.zeros_like(n_s)
        m_s[...] = jnp.zeros_like(m_s)
        qkbuf[0:SUBLANES, :] = jnp.zeros((SUBLANES, qkbuf.shape[1]), F32)
        _SideWork(_proj_pieces(x_ref, 0, tile, g_ref, nbuf, win_ref, zb0, 2 * PROJ_PIECE)).flush()

    mix = functools.partial(_odd_tile, dmask_ref=dmask_ref, cw_ref=cw_ref, cb_ref=cb_ref, gb_ref=gb_ref,
                            s_s=s_s, c_s=c_s, n_s=n_s, m_s=m_s, qkbuf=qkbuf, tile=tile)
    side = _SideWork(_proj_pieces(x_ref, tile, 2 * tile, g_ref, nbuf, win_ref, zb1, 2 * PROJ_PIECE),
                     ODD_TILE_STAGES)

    def rope(t):
        cb, sb = rbase_ref[0, 2 * t:2 * t + 1, :], rbase_ref[0, 2 * t + 1:2 * t + 2, :]
        return (cb * roff_ref[0] - sb * roff_ref[1], sb * roff_ref[2] + cb * roff_ref[3])

    mix(zb0, yb0, *rope(0), side)
    side.flush()
    side = _SideWork(_proj_pieces(xn_ref, 0, tile, g_ref, nbuf, win_ref, zb0, 2 * PROJ_PIECE)
                     + _out_pieces(yb0, wout_ref, x_ref, o_ref, 0, tile, PROJ_PIECE), ODD_TILE_STAGES)
    mix(zb1, yb1, *rope(1), side)
    side.flush()
    _SideWork(_out_pieces(yb1, wout_ref, x_ref, o_ref, tile, 2 * tile, PROJ_PIECE)).flush()
    conv_ref[...] = qkbuf[0:SUBLANES, :]
    so_ref[...] = s_s[...]
    co_ref[...] = c_s[...]
    no_ref[...] = n_s[...]
    mo_ref[...] = m_s[...]


def _odd_prompt(x, g, w_in, w_out, rbase, roff, dmask, cw, cb, gb):
    n, d = x.shape
    zin = w_in.shape[1]
    tile = dmask.shape[1]
    assert n % (2 * tile) == 0
    nt = n // tile
    wy = H_C * DV_C + H_D * DV_D
    wqk = 2 * H_D * DK_D
    small = (dmask, cw, cb, gb)
    return pl.pallas_call(
        functools.partial(_odd_prompt_body, tile=tile),
        grid=(nt // 2,),
        in_specs=[pl.BlockSpec((2 * tile, d), lambda i: (i, 0)),
                  pl.BlockSpec((tile, d), lambda i: (jnp.minimum(2 * i + 2, nt - 1), 0)),
                  _resident(g.shape), _resident(w_in.shape), _resident(w_out.shape),
                  pl.BlockSpec((1, 4, DK_C), lambda i: (i, 0, 0)), _resident(roff.shape)]
                 + [_resident(a.shape) for a in small],
        out_specs=[pl.BlockSpec((2 * tile, d), lambda i: (i, 0)),
                   pl.BlockSpec((SUBLANES, wqk), lambda i: (0, 0)),
                   pl.BlockSpec((H_C, DK_C, DV_C), lambda i: (0, 0, 0)),
                   pl.BlockSpec((H_D, DK_D, DV_D), lambda i: (0, 0, 0)),
                   pl.BlockSpec((SUBLANES, DK_D), lambda i: (0, 0)),
                   pl.BlockSpec((SUBLANES, LANES), lambda i: (0, 0))],
        out_shape=[jax.ShapeDtypeStruct((n, d), F32),
                   jax.ShapeDtypeStruct((SUBLANES, wqk), F32),
                   jax.ShapeDtypeStruct((H_C, DK_C, DV_C), F32),
                   jax.ShapeDtypeStruct((H_D, DK_D, DV_D), F32),
                   jax.ShapeDtypeStruct((SUBLANES, DK_D), F32),
                   jax.ShapeDtypeStruct((SUBLANES, LANES), F32)],
        scratch_shapes=[pltpu.VMEM((tile, zin), F32), pltpu.VMEM((tile, zin), F32),
                        pltpu.VMEM((tile, wy), BF16), pltpu.VMEM((tile, wy), BF16),
                        pltpu.VMEM((tile, d), BF16),
                        pltpu.VMEM((H_C, DK_C, DV_C), F32), pltpu.VMEM((H_D, DK_D, DV_D), F32),
                        pltpu.VMEM((SUBLANES, DK_D), F32), pltpu.VMEM((SUBLANES, LANES), F32),
                        pltpu.VMEM((tile + SUBLANES, wqk), F32)],
        compiler_params=_params(), name="odd_prompt")(x, x, g, w_in, w_out, rbase, roff, *small)


def _odd_sample_body(z_ref, cos_ref, sin_ref, s_ref, conv_ref, c_ref, n_ref, m_ref, cw_ref, cb_ref, gb_ref,
                     y_ref, so_ref, co_ref, no_ref, mo_ref, xc, *, sb, dl):
    wq = H_C * DK_C
    wv = H_C * DV_C
    o_kc, o_vc, o_gc = wq, 2 * wq, 2 * wq + wv
    o_qk = 2 * wq + 2 * wv
    wqk = 2 * H_D * DK_D
    o_vd = o_qk + wqk
    o_od = o_vd + H_D * DV_D
    o_gt = o_od + H_D * DV_D
    lg = _retention_log_decay()
    rows = sb * dl

    def z3(c0, width):
        return z_ref[:, c0:c0 + width].reshape(sb, dl, width)

    cosf = cos_ref[...][None]
    sinf = sin_ref[...][None]
    t1 = lax.broadcasted_iota(jnp.int32, (1, dl, 1), 1)
    tf = t1.astype(F32)

    for h in range(H_C):
        q = _rotate(z3(h * DK_C, DK_C), cosf, sinf, 2)
        k = _rotate(z3(o_kc + h * DK_C, DK_C), cosf, sinf, 2) * (DK_C ** -0.5)
        v = z3(o_vc + h * DV_C, DV_C)
        xi = jnp.exp((tf + 1.0) * lg[h])
        zeta = jnp.exp((dl - 1.0 - tf) * lg[h])
        s0 = s_ref[:, h]
        o = jnp.einsum('bqd,bde->bqe', (q * xi).astype(BF16), s0.astype(BF16), preferred_element_type=F32)
        for s in range(dl):
            ks = k if s == 0 else pltpu.roll(k, s, axis=1)
            vs = v if s == 0 else pltpu.roll(v, s, axis=1)
            coef = jnp.sum(q * ks, axis=-1, keepdims=True) * math.exp(s * lg[h])
            o = o + jnp.where(t1 >= s, coef, 0.0) * vs
        upd = jnp.einsum('btd,bte->bde', (k * zeta).astype(BF16), v.astype(BF16), preferred_element_type=F32)
        so_ref[:, h] = math.exp(dl * lg[h]) * s0 + upd
        gate = z3(o_gc + h * DV_C, DV_C)
        y_ref[:, h * DV_C:(h + 1) * DV_C] = (o * _rms_scale(o) * _silu(gate)).reshape(rows, DV_C).astype(BF16)

    xqk3 = z3(o_qk, wqk)
    xc[:, SUBLANES:SUBLANES + dl, :] = xqk3
    xc[:, SUBLANES - (CONV_W - 1):SUBLANES, :] = conv_ref[...]
    cw = cw_ref[...]
    qk = cb_ref[...] + cw[3:4] * xqk3
    for i in range(CONV_W - 1):
        off = SUBLANES - (CONV_W - 1) + i
        qk = qk + cw[i:i + 1] * xc[:, off:off + dl, :]
    qk = _silu(qk)

    gates = z3(o_gt, LANES) + gb_ref[...]
    logf = _log_sigmoid(gates)
    t = lax.broadcasted_iota(jnp.int32, gates.shape, 1)
    bsum = logf
    s = 1
    while s < dl:
        bsum = bsum + jnp.where(t >= s, pltpu.roll(bsum, s, axis=1), 0.0)
        s *= 2
    bsum = pltpu.roll(bsum, LANES - H_D, axis=2)
    m0 = m_ref[...]
    init_log = bsum + m0
    m_t = init_log
    dlogs = []
    for s in range(dl):
        if s == 0:
            d = gates
        else:
            d = bsum - pltpu.roll(bsum, s, axis=1) + pltpu.roll(gates, s, axis=1)
        d = jnp.where(t >= s, d, NEG)
        dlogs.append(d)
        m_t = jnp.maximum(m_t, d)
    a0 = jnp.exp(init_log - m_t)
    ws = [jnp.exp(d - m_t) for d in dlogs]
    inv_floor = jnp.exp(-m_t)
    b_end = bsum[:, dl - 1:dl, :]
    log_end = b_end - bsum + gates
    m_new = jnp.maximum(b_end + m0, jnp.max(log_end, axis=1, keepdims=True))
    w_end = jnp.exp(log_end - m_new)
    a_end = jnp.exp(b_end + m0 - m_new)
    mo_ref[...] = m_new
    for h in range(H_D):
        q = qk[:, :, h * DK_D:(h + 1) * DK_D]
        k = qk[:, :, H_D * DK_D + h * DK_D:H_D * DK_D + (h + 1) * DK_D] * (DK_D ** -0.5)
        v = z3(o_vd + h * DV_D, DV_D)
        c0 = c_ref[:, h]
        n0 = n_ref[:, h:h + 1, :]
        a0h = a0[:, :, h:h + 1]
        num = a0h * jnp.einsum('bqd,bde->bqe', q.astype(BF16), c0.astype(BF16), preferred_element_type=F32)
        den = a0h * jnp.sum(q * n0, axis=-1, keepdims=True)
        for s in range(dl):
            ks = k if s == 0 else pltpu.roll(k, s, axis=1)
            vs = v if s == 0 else pltpu.roll(v, s, axis=1)
            coef = jnp.sum(q * ks, axis=-1, keepdims=True) * ws[s][:, :, h:h + 1]
            num = num + coef * vs
            den = den + coef
        den = jnp.maximum(jnp.abs(den), inv_floor[:, :, h:h + 1])
        og = z3(o_od + h * DV_D, DV_D)
        y_ref[:, wv + h * DV_D:wv + (h + 1) * DV_D] = (
            (num / den) * _sigmoid(og)).reshape(rows, DV_D).astype(BF16)
        kw = k * w_end[:, :, h:h + 1]
        aeh = a_end[:, :, h:h + 1]
        upd = jnp.einsum('btd,bte->bde', kw.astype(BF16), v.astype(BF16), preferred_element_type=F32)
        co_ref[:, h] = aeh * c0 + upd
        no_ref[:, h:h + 1, :] = aeh * n0 + jnp.sum(kw, axis=1, keepdims=True)


def _odd_sample(z, cosf, sinf, s0, conv, c0, n0, m0, cw, cb, gb, *, dl):
    n, zin = z.shape
    db = n // dl
    sb = min(ODD_SB, db)
    assert db % sb == 0 and dl == SUBLANES
    rows = sb * dl
    wy = H_C * DV_C + H_D * DV_D
    wqk = 2 * H_D * DK_D
    st_spec = pl.BlockSpec((sb, H_C, DK_C, DV_C), lambda i: (i, 0, 0, 0))
    n_spec = pl.BlockSpec((sb, H_D, DK_D), lambda i: (i, 0, 0))
    m_spec = pl.BlockSpec((sb, 1, LANES), lambda i: (i, 0, 0))
    return pl.pallas_call(
        functools.partial(_odd_sample_body, sb=sb, dl=dl),
        grid=(db // sb,),
        in_specs=[pl.BlockSpec((rows, zin), lambda i: (i, 0)), _resident(cosf.shape), _resident(sinf.shape),
                  st_spec, pl.BlockSpec((sb, CONV_W - 1, wqk), lambda i: (i, 0, 0)), st_spec, n_spec, m_spec]
                 + [_resident(a.shape) for a in (cw, cb, gb)],
        out_specs=[pl.BlockSpec((rows, wy), lambda i: (i, 0)), st_spec, st_spec, n_spec, m_spec],
        out_shape=[jax.ShapeDtypeStruct((n, wy), BF16),
                   jax.ShapeDtypeStruct(s0.shape, F32), jax.ShapeDtypeStruct(c0.shape, F32),
                   jax.ShapeDtypeStruct(n0.shape, F32), jax.ShapeDtypeStruct(m0.shape, F32)],
        scratch_shapes=[pltpu.VMEM((sb, 2 * SUBLANES, wqk), F32)],
        compiler_params=_params(), name="odd_sample")(z, cosf, sinf, s0, conv, c0, n0, m0, cw, cb, gb)


def _t5_bucket(dist):
    n = np.maximum(dist, 0)
    max_exact = N_BUCKETS // 2
    large = max_exact + (np.log(np.maximum(n, max_exact) / max_exact)
                         / math.log(MAX_DIST / max_exact) * (N_BUCKETS - max_exact)).astype(np.int32)
    return np.where(n < max_exact, n, np.minimum(large, N_BUCKETS - 1)).astype(np.int32)


def _attn_bias(rel_bias, nq_rows, nkeys, p0):
    rel = np.arange(nq_rows)[:, None] + WINDOW - np.arange(nkeys)[None, :]
    kpos_ok = (p0 - WINDOW + np.arange(nkeys)) >= 0
    mask = (rel >= 0) & (rel < WINDOW) & kpos_ok[None, :]
    onehot = jnp.asarray(_t5_bucket(rel)[:, :, None] == np.arange(N_BUCKETS), F32)
    bias = jnp.einsum('qkb,bh->hqk', onehot, rel_bias.astype(F32), precision=lax.Precision.HIGHEST)
    bias = jnp.where(mask[None], bias, NEG)
    return bias.reshape(KV_B, G_B * nq_rows, nkeys)


def _sink_rows(sinks, nq_rows, width=1):
    col = jnp.repeat(sinks.astype(F32).reshape(KV_B, G_B), nq_rows, axis=1).reshape(KV_B, G_B * nq_rows, 1)
    return jnp.broadcast_to(col, (KV_B, G_B * nq_rows, width))


def _rope_tables(p0, n):
    half = DK_C // 2
    pos = p0 + jnp.arange(n, dtype=F32)
    inv = ROPE_BASE ** (-jnp.arange(half, dtype=F32) / half)
    ang = pos[:, None] * inv[None, :]
    cos, sin = jnp.cos(ang), jnp.sin(ang)
    return jnp.concatenate([cos, cos], axis=1), jnp.concatenate([-sin, sin], axis=1)


def _rope_split_tables(n, tile):
    half = DK_C // 2
    inv = ROPE_BASE ** (-jnp.arange(half, dtype=F32) / half)
    dup = lambda a: jnp.concatenate([a, a], axis=-1)
    base = (jnp.arange(n // tile, dtype=F32) * tile)[:, None] * inv[None, :]
    rbase = jnp.stack([dup(jnp.cos(base)), dup(jnp.sin(base))], axis=1).reshape(n // (2 * tile), 4, DK_C)
    off = jnp.arange(tile, dtype=F32)[:, None] * inv[None, :]
    sign = jnp.concatenate([-jnp.ones((half,), F32), jnp.ones((half,), F32)])
    co, so = dup(jnp.cos(off)), dup(jnp.sin(off))
    return rbase, jnp.stack([co, so, co * sign, so * sign])


def _decay_mask(tile):
    t = np.arange(tile)
    diff = (t[:, None] - t[None, :]).astype(np.float32)
    lg = np.asarray(_retention_log_decay(), np.float32)
    return jnp.asarray(np.where(diff >= 0, np.exp(np.maximum(diff, 0)[None] * lg[:, None, None]), 0.0), F32)


def _block_diag_gates(w_r, w_i, group):
    nb, c, _ = w_r.shape
    eye = jnp.eye(group, dtype=w_r.dtype)

    def bd(w):
        w4 = w.reshape(nb // group, group, c, c)
        return jnp.einsum('gncd,nm->gncmd', w4, eye).reshape(nb // group, group * c, group * c)

    return jnp.concatenate([bd(w_r), bd(w_i)], axis=2).astype(BF16)


def kernel(x_prompt, x_sample, state_a_conv, state_a_h, cache_b_k, cache_b_v, state_c_S, state_d_conv, state_d_C, state_d_n, state_d_m, norm_g, ffn1_w_in, ffn1_w_out, ffn2_w_in, ffn2_w_out, even_w_in, even_w_out, a_conv_w, a_conv_b, a_w_r, a_b_r, a_w_i, a_b_i, a_lambda, b_qk_norm, b_sinks, rel_bias, odd_w_in, odd_w_out, d_conv_w, d_conv_b, d_gate_b):
    bp, lp, d = x_prompt.shape
    db, dl, _ = x_sample.shape
    assert bp == 1, "prompt group is a single sequence"
    depth = norm_g.shape[0]
    w_a = a_conv_w.shape[2]
    nkv = KV_B * HD_B
    wqk = 2 * H_D * DK_D
    tile = min(MIX_TILE, lp // 2)

    xp = x_prompt.reshape(lp, d)
    xs = x_sample.reshape(db * dl, d)

    bias_p = _attn_bias(rel_bias, WINDOW, 2 * WINDOW, WINDOW)
    bias_s = _attn_bias(rel_bias, dl, WINDOW + dl, PAST_LEN)
    rbase_p, roff_p = _rope_split_tables(lp, tile)
    cos_s, sin_s = _rope_tables(float(PAST_LEN), dl)
    dmask = _decay_mask(tile)

    st = {}
    for li in range(depth):
        j = li // 2
        xs, wg, wu, wo = _ffn_cast(xs, norm_g[li, 0], ffn1_w_in, ffn1_w_out, li)
        xp = _ffn(xp, norm_g[li, 0], wg, wu, wo)
        if li % 2 == 0:
            zs, w_in = _norm_proj_cast(xs, norm_g[li, 1], even_w_in, j, EVEN_IN_CHUNK)
            shared = (a_conv_w[j], a_conv_b[j].reshape(1, w_a),
                      _block_diag_gates(a_w_r[j], a_w_i[j], 4),
                      jnp.stack([a_b_r[j], a_b_i[j]]), a_lambda[j].reshape(1, w_a),
                      b_qk_norm[j, 0].reshape(1, HD_B), jnp.tile(b_qk_norm[j, 1], KV_B).reshape(1, nkv))
            ys, hs_all, k_s, v_s = _even_sample(
                zs, state_a_conv[j], state_a_h[j], cache_b_k[j].reshape(db, WINDOW, nkv),
                cache_b_v[j].reshape(db, WINDOW, nkv), *shared, bias_s, _sink_rows(b_sinks[j], dl), dl=dl)
            xs, w_out = _out_proj_cast(ys, even_w_out, j, xs)
            seg = jnp.kron(jnp.eye(H_B, dtype=F32), jnp.ones((HD_B, HD_B), F32)).astype(BF16)
            xp, conv_p, h_p, k_p, v_p = _even_prompt(
                xp, norm_g[li, 1].reshape(1, d), w_in, w_out, *shared[:5],
                jnp.tile(b_qk_norm[j, 0], H_B).reshape(1, H_B * HD_B), shared[6], seg,
                bias_p, _sink_rows(b_sinks[j], WINDOW, LANES))
            st.setdefault('a_conv', ([], []))
            st['a_conv'][0].append(conv_p[SUBLANES - (CONV_W - 1):].reshape(1, CONV_W - 1, w_a))
            st['a_conv'][1].append(zs.reshape(db, dl, -1)[:, dl - (CONV_W - 1):, :w_a])
            st.setdefault('a_h', ([], []))
            st['a_h'][0].append(h_p[0:1])
            st['a_h'][1].append(hs_all.reshape(db, dl, w_a)[:, dl - 1])
            st.setdefault('b_k', ([], []))
            st['b_k'][0].append(k_p.reshape(1, WINDOW, KV_B, HD_B))
            st['b_k'][1].append(k_s.reshape(db, WINDOW, KV_B, HD_B))
            st.setdefault('b_v', ([], []))
            st['b_v'][0].append(v_p.reshape(1, WINDOW, KV_B, HD_B))
            st['b_v'][1].append(v_s.reshape(db, WINDOW, KV_B, HD_B))
        else:
            zs, w_in = _norm_proj_cast(xs, norm_g[li, 1], odd_w_in, j, ODD_IN_CHUNK)
            o_qk = 2 * H_C * DK_C + 2 * H_C * DV_C
            gb = jnp.pad(d_gate_b[j], (0, LANES - 2 * H_D)).reshape(1, LANES)
            shared = (d_conv_w[j], d_conv_b[j].reshape(1, wqk), gb)
            m0 = jnp.pad(state_d_m[j], ((0, 0), (0, LANES - H_D))).reshape(db, 1, LANES)
            ys, s_s, c_s, n_s, m_s = _odd_sample(zs, cos_s, sin_s, state_c_S[j], state_d_conv[j],
                                                 state_d_C[j], state_d_n[j], m0, *shared, dl=dl)
            xs, w_out = _out_proj_cast(ys, odd_w_out, j, xs)
            xp, dconv_p, s_p, c_p, n_p, m_p = _odd_prompt(xp, norm_g[li, 1].reshape(1, d), w_in, w_out,
                                                          rbase_p, roff_p, dmask, *shared)
            st.setdefault('c_S', ([], []))
            st['c_S'][0].append(s_p[None])
            st['c_S'][1].append(s_s)
            st.setdefault('d_conv', ([], []))
            st['d_conv'][0].append(dconv_p[SUBLANES - (CONV_W - 1):].reshape(1, CONV_W - 1, wqk))
            st['d_conv'][1].append(zs.reshape(db, dl, -1)[:, dl - (CONV_W - 1):, o_qk:o_qk + wqk])
            st.setdefault('d_C', ([], []))
            st['d_C'][0].append(c_p[None])
            st['d_C'][1].append(c_s)
            st.setdefault('d_n', ([], []))
            st['d_n'][0].append(n_p[None, :H_D])
            st['d_n'][1].append(n_s)
            st.setdefault('d_m', ([], []))
            st['d_m'][0].append(m_p[:H_D, 0].reshape(1, H_D))
            st['d_m'][1].append(m_s[:, 0, :H_D])
        xs, wg, wu, wo = _ffn_cast(xs, norm_g[li, 2], ffn2_w_in, ffn2_w_out, li)
        xp = _ffn(xp, norm_g[li, 2], wg, wu, wo)

    outs = [xp.reshape(1, lp, d), xs.reshape(db, dl, d)]
    for name in ('a_conv', 'a_h', 'b_k', 'b_v', 'c_S', 'd_conv', 'd_C', 'd_n', 'd_m'):
        outs.append(jnp.stack(st[name][0]))
        outs.append(jnp.stack(st[name][1]))
    return tuple(outs)
```

```python
import functools
import math

import jax
import jax.numpy as jnp
import numpy as np
from jax import lax
from jax.experimental import pallas as pl
from jax.experimental.pallas import tpu as pltpu

F32 = jnp.float32
BF16 = jnp.bfloat16

PAST_LEN = 16384
EPS = 1e-6
CONV_W = 4
C_A = 8.0
NB_A = 16
H_B, KV_B, HD_B = 8, 2, 64
G_B = H_B // KV_B
WINDOW = 128
N_BUCKETS = 32
MAX_DIST = 128
H_C, DK_C, DV_C = 4, 128, 256
H_D, DK_D, DV_D = 4, 128, 256
ROPE_BASE = 10000.0
NEG = -1e30

LANES = 128
SUBLANES = 8
VMEM_LIMIT = 56 * 1024 * 1024
VMEM_LIMIT_MAX = 60000 * 1024

ROW_TILE = 512
FFN_TILE = 1024
MIX_TILE = 256
FF_CHUNK = 256
EVEN_SB = 32
ODD_SB = 8
PROJ_PIECE = 256
EVEN_IN_CHUNK = 1408
ODD_IN_CHUNK = 896


def _params(n_grid_dims=1, vmem_limit=VMEM_LIMIT):
    return pltpu.CompilerParams(dimension_semantics=("arbitrary",) * n_grid_dims,
                                vmem_limit_bytes=vmem_limit)


def _resident(shape):
    nd = len(shape)
    return pl.BlockSpec(shape, lambda i, _nd=nd: (0,) * _nd, pipeline_mode=pl.Buffered(1))


def _dot(a, b):
    return jnp.dot(a.astype(BF16), b.astype(BF16), preferred_element_type=F32)


def _dot_nt(a, b):
    return lax.dot_general(a.astype(BF16), b.astype(BF16), (((1,), (1,)), ((), ())),
                           preferred_element_type=F32)


def _dot_tn(a, b):
    return lax.dot_general(a.astype(BF16), b.astype(BF16), (((0,), (0,)), ((), ())),
                           preferred_element_type=F32)


def _rms_scale(x):
    return lax.rsqrt(jnp.mean(x * x, axis=-1, keepdims=True) + EPS)


def _sigmoid(x):
    return 1.0 / (1.0 + jnp.exp(-x))


def _silu(x):
    return x * _sigmoid(x)


def _log_sigmoid(x):
    return jnp.minimum(x, 0.0) - jnp.log1p(jnp.exp(-jnp.abs(x)))


def _gelu_tanh(x):
    k = 2.0 * math.sqrt(2.0 / math.pi)
    return x / (1.0 + jnp.exp(x * (-k - (k * 0.044715) * (x * x))))


def _sqrt_nonneg(x):
    return jnp.where(x > 0.0, x * lax.rsqrt(x), 0.0)


def _ffn_body(x_ref, g_ref, wg_ref, wu_ref, wo_ref, o_ref, *, d_ff, chunk):
    x = x_ref[...]
    nb = (x * _rms_scale(x) * g_ref[...]).astype(BF16)
    acc = None
    for c0 in range(0, d_ff, chunk):
        gate = jnp.dot(nb, wg_ref[:, c0:c0 + chunk], preferred_element_type=F32)
        up = jnp.dot(nb, wu_ref[:, c0:c0 + chunk], preferred_element_type=F32)
        mid = (_silu(gate) * up).astype(BF16)
        part = jnp.dot(mid, wo_ref[c0:c0 + chunk, :], preferred_element_type=F32)
        acc = part if acc is None else acc + part
    o_ref[...] = x + 0.5 * acc


def _ffn(x, g, wg, wu, wo):
    n, d = x.shape
    d_ff = wo.shape[0]
    tm = min(FFN_TILE, n)
    assert n % tm == 0 and d_ff % FF_CHUNK == 0
    return pl.pallas_call(
        functools.partial(_ffn_body, d_ff=d_ff, chunk=FF_CHUNK),
        grid=(n // tm,),
        in_specs=[pl.BlockSpec((tm, d), lambda i: (i, 0)), _resident((1, d)),
                  _resident(wg.shape), _resident(wu.shape), _resident(wo.shape)],
        out_specs=pl.BlockSpec((tm, d), lambda i: (i, 0)),
        out_shape=jax.ShapeDtypeStruct((n, d), F32),
        compiler_params=_params(), name="ffn")(x, g.reshape(1, d), wg, wu, wo)


def _ffn_cast_body(x_ref, g_ref, wg_ref, wu_ref, wo_ref, o_ref, wgb_ref, wub_ref, wob_ref, nb_ref, acc_ref):
    j = pl.program_id(0)

    @pl.when(j == 0)
    def _():
        x = x_ref[...]
        nb_ref[...] = (x * _rms_scale(x) * g_ref[...]).astype(BF16)
        acc_ref[...] = jnp.zeros_like(acc_ref)

    wg = wg_ref[...].astype(BF16)
    wu = wu_ref[...].astype(BF16)
    wo = wo_ref[...].astype(BF16)
    wgb_ref[...] = wg
    wub_ref[...] = wu
    wob_ref[...] = wo
    nb = nb_ref[...]
    gate = jnp.dot(nb, wg, preferred_element_type=F32)
    up = jnp.dot(nb, wu, preferred_element_type=F32)
    acc_ref[...] += jnp.dot((_silu(gate) * up).astype(BF16), wo, preferred_element_type=F32)

    @pl.when(j == pl.num_programs(0) - 1)
    def _():
        o_ref[...] = x_ref[...] + 0.5 * acc_ref[...]


def _ffn_cast(x, g, w_in, w_out, li):
    n, d = x.shape
    d_ff = w_out.shape[1]
    nc = d_ff // FF_CHUNK
    assert d_ff % FF_CHUNK == 0
    return pl.pallas_call(
        _ffn_cast_body, grid=(nc,),
        in_specs=[pl.BlockSpec((n, d), lambda j: (0, 0)), pl.BlockSpec((1, d), lambda j: (0, 0)),
                  pl.BlockSpec((None, d, FF_CHUNK), lambda j: (li, 0, j)),
                  pl.BlockSpec((None, d, FF_CHUNK), lambda j: (li, 0, j + nc)),
                  pl.BlockSpec((None, FF_CHUNK, d), lambda j: (li, j, 0))],
        out_specs=[pl.BlockSpec((n, d), lambda j: (0, 0)),
                   pl.BlockSpec((d, FF_CHUNK), lambda j: (0, j)),
                   pl.BlockSpec((d, FF_CHUNK), lambda j: (0, j)),
                   pl.BlockSpec((FF_CHUNK, d), lambda j: (j, 0))],
        out_shape=[jax.ShapeDtypeStruct((n, d), F32), jax.ShapeDtypeStruct((d, d_ff), BF16),
                   jax.ShapeDtypeStruct((d, d_ff), BF16), jax.ShapeDtypeStruct((d_ff, d), BF16)],
        scratch_shapes=[pltpu.VMEM((n, d), BF16), pltpu.VMEM((n, d), F32)],
        compiler_params=_params(), name="ffn_cast")(x, g.reshape(1, d), w_in, w_in, w_out)


def _norm_proj_cast_body(x_ref, g_ref, w_ref, z_ref, wb_ref, nb_ref, *, chunk, valid):
    j = pl.program_id(0)

    @pl.when(j == 0)
    def _():
        x = x_ref[...]
        nb_ref[...] = (x * _rms_scale(x) * g_ref[...]).astype(BF16)

    col = j * chunk + lax.broadcasted_iota(jnp.int32, w_ref.shape, 1)
    w = jnp.where(col < valid, w_ref[...], 0.0).astype(BF16)
    wb_ref[...] = w
    z_ref[...] = jnp.dot(nb_ref[...], w, preferred_element_type=F32)


def _norm_proj_cast(x, g, w_all, li, chunk):
    n, d = x.shape
    m = w_all.shape[2]
    nc = -(-m // chunk)
    mp = nc * chunk
    return pl.pallas_call(
        functools.partial(_norm_proj_cast_body, chunk=chunk, valid=m), grid=(nc,),
        in_specs=[pl.BlockSpec((n, d), lambda j: (0, 0)), pl.BlockSpec((1, d), lambda j: (0, 0)),
                  pl.BlockSpec((None, d, chunk), lambda j: (li, 0, j))],
        out_specs=[pl.BlockSpec((n, chunk), lambda j: (0, j)), pl.BlockSpec((d, chunk), lambda j: (0, j))],
        out_shape=[jax.ShapeDtypeStruct((n, mp), F32), jax.ShapeDtypeStruct((d, mp), BF16)],
        scratch_shapes=[pltpu.VMEM((n, d), BF16)],
        compiler_params=_params(), name="norm_proj_cast")(x, g.reshape(1, d), w_all)


def _out_proj_cast_body(y_ref, w_ref, x_ref, o_ref, wb_ref):
    w = w_ref[...].astype(BF16)
    wb_ref[...] = w
    o_ref[...] = x_ref[...] + jnp.dot(y_ref[...], w, preferred_element_type=F32)


def _out_proj_cast(y, w_all, li, x):
    n, k = y.shape
    d = w_all.shape[2]
    chunk = PROJ_PIECE
    assert d % chunk == 0
    return pl.pallas_call(
        _out_proj_cast_body, grid=(d // chunk,),
        in_specs=[pl.BlockSpec((n, k), lambda j: (0, 0)),
                  pl.BlockSpec((None, k, chunk), lambda j: (li, 0, j)),
                  pl.BlockSpec((n, chunk), lambda j: (0, j))],
        out_specs=[pl.BlockSpec((n, chunk), lambda j: (0, j)), pl.BlockSpec((k, chunk), lambda j: (0, j))],
        out_shape=[jax.ShapeDtypeStruct((n, d), F32), jax.ShapeDtypeStruct((k, d), BF16)],
        compiler_params=_params(), name="out_proj_cast")(y, w_all, x)


class _SideWork:
    def __init__(self, pieces=(), stages=1):
        self.pieces, self.stages, self.done, self.stage = list(pieces), stages, 0, 0

    def tick(self):
        self.stage += 1
        upto = min(len(self.pieces), -(-len(self.pieces) * self.stage // self.stages))
        while self.done < upto:
            self.pieces[self.done]()
            self.done += 1

    def flush(self):
        self.stage = self.stages - 1
        self.tick()


def _norm_piece(x_ref, r0, r1, g_ref, nb_ref):
    x = x_ref[r0:r1, :]
    nb_ref[...] = (x * _rms_scale(x) * g_ref[...]).astype(BF16)


def _proj_piece(nb_ref, w_ref, z_ref, c0, c1):
    z_ref[:, c0:c1] = jnp.dot(nb_ref[...], w_ref[:, c0:c1], preferred_element_type=F32)


def _proj_pieces(x_ref, r0, r1, g_ref, nb_ref, w_ref, z_ref, width):
    m = w_ref.shape[1]
    return [functools.partial(_norm_piece, x_ref, r0, r1, g_ref, nb_ref)] + [
        functools.partial(_proj_piece, nb_ref, w_ref, z_ref, c0, min(c0 + width, m))
        for c0 in range(0, m, width)]


def _out_piece(y_ref, y_val, w_ref, x_ref, xr0, o_ref, or0, rows, c0, c1):
    if not y_val:
        y_val.append(y_ref[...])
    o_ref[or0:or0 + rows, c0:c1] = x_ref[xr0:xr0 + rows, c0:c1] + jnp.dot(
        y_val[0], w_ref[:, c0:c1], preferred_element_type=F32)


def _out_pieces(y_ref, w_ref, x_ref, xr0, o_ref, or0, rows, width):
    m = w_ref.shape[1]
    y_val = []
    return [functools.partial(_out_piece, y_ref, y_val, w_ref, x_ref, xr0, o_ref, or0, rows,
                              c0, min(c0 + width, m)) for c0 in range(0, m, width)]


def _ffn_up_piece(nb_ref, wg_ref, wu_ref, mid_ref, c0, c1):
    nb = nb_ref[...]
    gate = jnp.dot(nb, wg_ref[:, c0:c1], preferred_element_type=F32)
    up = jnp.dot(nb, wu_ref[:, c0:c1], preferred_element_type=F32)
    mid_ref[...] = (_silu(gate) * up).astype(BF16)


def _ffn_down_piece(mid_ref, wo_ref, acc_ref, c0, c1):
    part = jnp.dot(mid_ref[...], wo_ref[c0:c1, :], preferred_element_type=F32)
    if c0 == 0:
        acc_ref[...] = part
    else:
        acc_ref[...] += part


def _ffn_final_piece(x_ref, r0, r1, acc_ref, o_ref):
    o_ref[...] = x_ref[r0:r1, :] + 0.5 * acc_ref[...]


def _ffn_pieces(x_ref, r0, r1, g_ref, nb_ref, wg_ref, wu_ref, wo_ref, mid_refs, acc_ref, o_ref, width):
    d_ff = wo_ref.shape[0]
    assert d_ff % width == 0
    ups = [functools.partial(_ffn_up_piece, nb_ref, wg_ref, wu_ref, mid_refs[(c0 // width) % 2], c0, c0 + width)
           for c0 in range(0, d_ff, width)]
    downs = [functools.partial(_ffn_down_piece, mid_refs[(c0 // width) % 2], wo_ref, acc_ref, c0, c0 + width)
             for c0 in range(0, d_ff, width)]
    order = [ups[0]]
    for c in range(1, len(ups)):
        order += [ups[c], downs[c - 1]]
    order.append(downs[-1])
    return ([functools.partial(_norm_piece, x_ref, r0, r1, g_ref, nb_ref)] + order
            + [functools.partial(_ffn_final_piece, x_ref, r0, r1, acc_ref, o_ref)])


def _rglru_gate_groups(y, wri_ref, bri_ref, lam_ref):
    w = y.shape[1]
    gw = wri_ref.shape[1]
    yb = y.astype(BF16)
    logsig = _log_sigmoid(lam_ref[...])
    for g in range(w // gw):
        cols = slice(g * gw, (g + 1) * gw)
        ri = jnp.dot(yb[:, cols], wri_ref[g], preferred_element_type=F32)
        r = _sigmoid(ri[:, :gw] + bri_ref[0:1, cols])
        i = _sigmoid(ri[:, gw:] + bri_ref[1:2, cols])
        log_a = C_A * r * logsig[:, cols]
        th = jnp.tanh(log_a)
        one_minus_a2 = -2.0 * th / (1.0 - th)
        yield jnp.exp(log_a), _sqrt_nonneg(one_minus_a2) * (i * y[:, cols])


def _rglru_gates(y, wri_ref, bri_ref, lam_ref):
    parts = list(_rglru_gate_groups(y, wri_ref, bri_ref, lam_ref))
    return jnp.concatenate([p[0] for p in parts], axis=1), jnp.concatenate([p[1] for p in parts], axis=1)


def _group_scan(a3, u3):
    t = lax.broadcasted_iota(jnp.int32, a3.shape, 1)
    s = 1
    while s < SUBLANES:
        keep = t >= s
        u3 = jnp.where(keep, a3 * pltpu.roll(u3, s, axis=1) + u3, u3)
        a3 = jnp.where(keep, a3 * pltpu.roll(a3, s, axis=1), a3)
        s *= 2
    return a3, u3


EVEN_TILE_STAGES = 16


def _even_rglru_branch(z_ref, y_ref, cw_ref, cb_ref, wri_ref, bri_ref, lam_ref, conv_ref, h_ref,
                       xbuf, hcar, hbuf, *, tile, w_a):
    xa = z_ref[:, 0:w_a]
    xbuf[SUBLANES:SUBLANES + tile, :] = xa
    cw = cw_ref[...]
    y = cb_ref[...] + cw[3:4] * xa
    for i in range(CONV_W - 1):
        off = SUBLANES - (CONV_W - 1) + i
        y = y + cw[i:i + 1] * xbuf[off:off + tile, :]
    xbuf[0:SUBLANES, :] = xbuf[tile:tile + SUBLANES, :]
    conv_ref[...] = xbuf[0:SUBLANES, :]
    yield

    a_parts, u_parts = [], []
    for a_g, u_g in _rglru_gate_groups(y, wri_ref, bri_ref, lam_ref):
        a_parts.append(a_g)
        u_parts.append(u_g)
        yield
    a = jnp.concatenate(a_parts, axis=1)
    u = jnp.concatenate(u_parts, axis=1)

    ng = tile // SUBLANES
    a3, u3 = _group_scan(a.reshape(ng, SUBLANES, w_a), u.reshape(ng, SUBLANES, w_a))
    yield
    carry = hcar[0:1, :]
    for g in range(ng):
        hg = u3[g] + a3[g] * carry
        hbuf[g * SUBLANES:(g + 1) * SUBLANES, :] = hg
        carry = hg[SUBLANES - 1:SUBLANES, :]
    hcar[0:1, :] = carry
    h_ref[...] = jnp.broadcast_to(carry, h_ref.shape)
    yield
    y_ref[:, 0:w_a] = (hbuf[...] * _gelu_tanh(z_ref[:, w_a:2 * w_a])).astype(BF16)
    yield


def _head_mean_sq(x, seg):
    x2 = x * x
    hi = x2.astype(BF16)
    lo = (x2 - hi.astype(F32)).astype(BF16)
    return (jnp.dot(hi, seg, preferred_element_type=F32)
            + jnp.dot(lo, seg, preferred_element_type=F32)) * (1.0 / HD_B)


def _even_attn_branch(z_ref, y_ref, is_first, qg_ref, kg_ref, seg_ref, bias_ref, sink_ref, kl_ref, vl_ref,
                      kbuf, vbuf, *, tile, w_a):
    nq = H_B * HD_B
    nkv = KV_B * HD_B
    k = z_ref[:, 2 * w_a + nq:2 * w_a + nq + nkv]
    kbuf[WINDOW:WINDOW + tile, :] = k * lax.rsqrt(_head_mean_sq(k, seg_ref[0:nkv, 0:nkv]) + EPS) * kg_ref[...]
    vbuf[WINDOW:WINDOW + tile, :] = z_ref[:, 2 * w_a + nq + nkv:2 * w_a + nq + 2 * nkv]
    yield

    chains = [(nb, kv) for nb in range(tile // WINDOW) for kv in range(KV_B)]
    rows = G_B * WINDOW
    q_all = z_ref[:, 2 * w_a:2 * w_a + nq]
    q_all = q_all * lax.rsqrt(_head_mean_sq(q_all, seg_ref[...]) + EPS) * qg_ref[...] * (HD_B ** -0.5)
    qs = jnp.concatenate([q_all[nb * WINDOW:(nb + 1) * WINDOW, (kv * G_B + g) * HD_B:(kv * G_B + g + 1) * HD_B]
                          for nb, kv in chains for g in range(G_B)], axis=0).astype(BF16)
    yield
    col = lax.broadcasted_iota(jnp.int32, (rows, 2 * WINDOW), 1)
    s_parts = []
    for c, (nb, kv) in enumerate(chains):
        kk = kbuf[nb * WINDOW:(nb + 2) * WINDOW, kv * HD_B:(kv + 1) * HD_B]
        s = _dot_nt(qs[c * rows:(c + 1) * rows], kk) + bias_ref[kv]
        if nb == 0 and is_first is not False:
            s = jnp.where(jnp.logical_and(is_first, col < WINDOW), NEG, s)
        s_parts.append(s)
        yield
    s = jnp.concatenate(s_parts, axis=0)
    sink = jnp.concatenate([sink_ref[kv] for _, kv in chains], axis=0)
    m = jnp.maximum(jnp.max(s, axis=-1, keepdims=True), sink)
    p = jnp.exp(s - jnp.concatenate([m] * (2 * WINDOW // LANES), axis=1)).astype(BF16)
    psum = jnp.dot(p, jnp.ones((2 * WINDOW, LANES), BF16), preferred_element_type=F32)
    inv = (1.0 / (psum + jnp.exp(sink - m)))[:, 0:HD_B]
    yield
    o = jnp.concatenate([_dot(p[c * rows:(c + 1) * rows],
                              vbuf[nb * WINDOW:(nb + 2) * WINDOW, kv * HD_B:(kv + 1) * HD_B])
                         for c, (nb, kv) in enumerate(chains)], axis=0) * inv
    y_ref[:, w_a:w_a + nq] = jnp.concatenate(
        [jnp.concatenate([o[(c * G_B + g) * WINDOW:(c * G_B + g + 1) * WINDOW, :]
                          for c, (cb_, _) in enumerate(chains) if cb_ == nb for g in range(G_B)], axis=1)
         for nb in range(tile // WINDOW)], axis=0).astype(BF16)

    kbuf[0:WINDOW, :] = kbuf[tile:tile + WINDOW, :]
    vbuf[0:WINDOW, :] = vbuf[tile:tile + WINDOW, :]
    kl_ref[...] = kbuf[0:WINDOW, :]
    vl_ref[...] = vbuf[0:WINDOW, :]
    yield


def _even_tile(z_ref, y_ref, is_first, side, cw_ref, cb_ref, wri_ref, bri_ref, lam_ref, qg_ref, kg_ref,
               seg_ref, bias_ref, sink_ref, conv_ref, h_ref, kl_ref, vl_ref, xbuf, hcar, hbuf, kbuf, vbuf, *, tile, w_a):
    branches = [
        _even_rglru_branch(z_ref, y_ref, cw_ref, cb_ref, wri_ref, bri_ref, lam_ref, conv_ref, h_ref,
                           xbuf, hcar, hbuf, tile=tile, w_a=w_a),
        _even_attn_branch(z_ref, y_ref, is_first, qg_ref, kg_ref, seg_ref, bias_ref, sink_ref, kl_ref, vl_ref,
                          kbuf, vbuf, tile=tile, w_a=w_a)]
    while branches:
        for b in list(branches):
            if next(b, StopIteration) is StopIteration:
                branches.remove(b)
            else:
                side.tick()


def _even_prompt_body(x_ref, xn_ref, gf_ref, wg_ref, wu_ref, wo_ref, g_ref, win_ref, wout_ref, cw_ref, cb_ref,
                      wri_ref, bri_ref, lam_ref, qg_ref, kg_ref, seg_ref, bias_ref, sink_ref,
                      o_ref, conv_ref, h_ref, kl_ref, vl_ref,
                      zb0, zb1, yb0, yb1, x1b0, x1b1, acc, mid0, mid1, nbuf, xbuf, hcar, hbuf, kbuf, vbuf, *,
                      tile, w_a):
    step = pl.program_id(0)
    nkv = KV_B * HD_B

    def ffn_proj(src_ref, r0, x1b, zb):
        return (_ffn_pieces(src_ref, r0, r0 + tile, gf_ref, nbuf, wg_ref, wu_ref, wo_ref, (mid0, mid1), acc, x1b,
                            FF_CHUNK)
                + _proj_pieces(x1b, 0, tile, g_ref, nbuf, win_ref, zb, PROJ_PIECE))

    @pl.when(step == 0)
    def _():
        xbuf[0:SUBLANES, :] = jnp.zeros((SUBLANES, w_a), F32)
        hcar[...] = jnp.zeros_like(hcar)
        kbuf[0:WINDOW, :] = jnp.zeros((WINDOW, nkv), F32)
        vbuf[0:WINDOW, :] = jnp.zeros((WINDOW, nkv), F32)
        _SideWork(ffn_proj(x_ref, 0, x1b0, zb0)).flush()

    mix = functools.partial(_even_tile, cw_ref=cw_ref, cb_ref=cb_ref, wri_ref=wri_ref, bri_ref=bri_ref,
                            lam_ref=lam_ref, qg_ref=qg_ref, kg_ref=kg_ref, seg_ref=seg_ref, bias_ref=bias_ref,
                            sink_ref=sink_ref, conv_ref=conv_ref, h_ref=h_ref, kl_ref=kl_ref, vl_ref=vl_ref,
                            xbuf=xbuf, hcar=hcar, hbuf=hbuf, kbuf=kbuf, vbuf=vbuf, tile=tile, w_a=w_a)
    side = _SideWork(ffn_proj(x_ref, tile, x1b1, zb1), EVEN_TILE_STAGES)
    mix(zb0, yb0, step == 0, side)
    side.flush()
    side = _SideWork(_out_pieces(yb0, wout_ref, x1b0, 0, o_ref, 0, tile, PROJ_PIECE)
                     + ffn_proj(xn_ref, 0, x1b0, zb0), EVEN_TILE_STAGES)
    mix(zb1, yb1, False, side)
    side.flush()
    _SideWork(_out_pieces(yb1, wout_ref, x1b1, 0, o_ref, tile, tile, PROJ_PIECE)).flush()


def _even_prompt(x, gf, wg, wu, wo, g, w_in, w_out, cw, cb, wri, bri, lam, qg, kg, seg, bias, sink):
    n, d = x.shape
    zin = w_in.shape[1]
    w_a = cw.shape[1]
    tile = min(MIX_TILE, n // 2)
    assert n % (2 * tile) == 0 and tile % WINDOW == 0
    nt = n // tile
    nq, nkv = H_B * HD_B, KV_B * HD_B
    small = (gf, wg, wu, wo, g, w_in, w_out, cw, cb, wri, bri, lam, qg, kg, seg, bias, sink)
    return pl.pallas_call(
        functools.partial(_even_prompt_body, tile=tile, w_a=w_a),
        grid=(nt // 2,),
        in_specs=[pl.BlockSpec((2 * tile, d), lambda i: (i, 0)),
                  pl.BlockSpec((tile, d), lambda i: (jnp.minimum(2 * i + 2, nt - 1), 0))]
                 + [_resident(a.shape) for a in small],
        out_specs=[pl.BlockSpec((2 * tile, d), lambda i: (i, 0)),
                   pl.BlockSpec((SUBLANES, w_a), lambda i: (0, 0)),
                   pl.BlockSpec((SUBLANES, w_a), lambda i: (0, 0)),
                   pl.BlockSpec((WINDOW, nkv), lambda i: (0, 0)),
                   pl.BlockSpec((WINDOW, nkv), lambda i: (0, 0))],
        out_shape=[jax.ShapeDtypeStruct((n, d), F32),
                   jax.ShapeDtypeStruct((SUBLANES, w_a), F32),
                   jax.ShapeDtypeStruct((SUBLANES, w_a), F32),
                   jax.ShapeDtypeStruct((WINDOW, nkv), F32),
                   jax.ShapeDtypeStruct((WINDOW, nkv), F32)],
        scratch_shapes=[pltpu.VMEM((tile, zin), F32), pltpu.VMEM((tile, zin), F32),
                        pltpu.VMEM((tile, w_a + nq), BF16), pltpu.VMEM((tile, w_a + nq), BF16),
                        pltpu.VMEM((tile, d), F32), pltpu.VMEM((tile, d), F32), pltpu.VMEM((tile, d), F32),
                        pltpu.VMEM((tile, FF_CHUNK), BF16), pltpu.VMEM((tile, FF_CHUNK), BF16),
                        pltpu.VMEM((tile, d), BF16),
                        pltpu.VMEM((tile + SUBLANES, w_a), F32), pltpu.VMEM((SUBLANES, w_a), F32),
                        pltpu.VMEM((tile, w_a), F32),
                        pltpu.VMEM((tile + WINDOW, nkv), F32), pltpu.VMEM((tile + WINDOW, nkv), F32)],
        compiler_params=_params(vmem_limit=VMEM_LIMIT_MAX), name="even_prompt")(x, x, *small)


def _even_sample_body(z_ref, conv_ref, h0_ref, ck_ref, cv_ref, cw_ref, cb_ref, wri_ref, bri_ref, lam_ref,
                      qg_ref, kg_ref, bias_ref, sink_ref, y_ref, hs_ref, ko_ref, vo_ref, xc, *, sb, dl, w_a):
    nq = H_B * HD_B
    nkv = KV_B * HD_B
    rows = sb * dl
    nkeys = WINDOW + dl

    xa3 = z_ref[:, 0:w_a].reshape(sb, dl, w_a)
    xc[:, SUBLANES:SUBLANES + dl, :] = xa3
    xc[:, SUBLANES - (CONV_W - 1):SUBLANES, :] = conv_ref[...]
    cw = cw_ref[...]
    y3 = cb_ref[...] + cw[3:4] * xa3
    for i in range(CONV_W - 1):
        off = SUBLANES - (CONV_W - 1) + i
        y3 = y3 + cw[i:i + 1] * xc[:, off:off + dl, :]
    y = y3.reshape(rows, w_a)

    a, u = _rglru_gates(y, wri_ref, bri_ref, lam_ref)
    a3 = a.reshape(sb, dl, w_a)
    u3 = u.reshape(sb, dl, w_a)
    t = lax.broadcasted_iota(jnp.int32, a3.shape, 1)
    h0 = jnp.broadcast_to(h0_ref[...][:, None, :], a3.shape)
    u3 = jnp.where(t == 0, u3 + a3 * h0, u3)
    _, h3 = _group_scan(a3, u3)
    hs = h3.reshape(rows, w_a)
    hs_ref[...] = hs
    y_ref[:, 0:w_a] = (hs * _gelu_tanh(z_ref[:, w_a:2 * w_a])).astype(BF16)

    q3 = z_ref[:, 2 * w_a:2 * w_a + nq].reshape(sb, dl, nq)
    k3 = z_ref[:, 2 * w_a + nq:2 * w_a + nq + nkv].reshape(sb, dl, nkv)
    v3 = z_ref[:, 2 * w_a + nq + nkv:2 * w_a + nq + 2 * nkv].reshape(sb, dl, nkv)
    kparts = []
    for hh in range(KV_B):
        kh = k3[:, :, hh * HD_B:(hh + 1) * HD_B]
        kparts.append(kh * _rms_scale(kh))
    kn3 = jnp.concatenate(kparts, axis=2) * kg_ref[...]
    ko_ref[:, 0:WINDOW - dl, :] = ck_ref[:, dl:WINDOW, :]
    ko_ref[:, WINDOW - dl:WINDOW, :] = kn3
    vo_ref[:, 0:WINDOW - dl, :] = cv_ref[:, dl:WINDOW, :]
    vo_ref[:, WINDOW - dl:WINDOW, :] = v3

    outs = []
    for kv in range(KV_B):
        hs_ = slice(kv * HD_B, (kv + 1) * HD_B)
        qs = jnp.concatenate([q3[:, :, (kv * G_B + g) * HD_B:(kv * G_B + g + 1) * HD_B]
                              for g in range(G_B)], axis=1)
        qs = qs * _rms_scale(qs) * qg_ref[...] * (HD_B ** -0.5)
        kc = jnp.concatenate([ck_ref[:, :, hs_], kn3[:, :, hs_]], axis=1)
        vc = jnp.concatenate([cv_ref[:, :, hs_], v3[:, :, hs_]], axis=1)
        s = jnp.einsum('bqd,bkd->bqk', qs.astype(BF16), kc.astype(BF16),
                       preferred_element_type=F32) + bias_ref[kv]
        sink = sink_ref[kv]
        m = jnp.maximum(jnp.max(s, axis=-1, keepdims=True), sink)
        p = jnp.exp(s - m)
        den = jnp.sum(p, axis=-1, keepdims=True) + jnp.exp(sink - m)
        o = jnp.einsum('bqk,bkd->bqd', p.astype(BF16), vc.astype(BF16),
                       preferred_element_type=F32) / den
        outs += [o[:, g * dl:(g + 1) * dl, :] for g in range(G_B)]
    y_ref[:, w_a:w_a + nq] = jnp.concatenate(outs, axis=2).reshape(rows, nq).astype(BF16)


def _even_sample(z, conv, h0, ck, cv, cw, cb, wri, bri, lam, qg, kg, bias, sink, *, dl):
    n, zin = z.shape
    db = n // dl
    w_a = cw.shape[1]
    sb = min(EVEN_SB, db)
    assert db % sb == 0 and dl == SUBLANES
    nq, nkv = H_B * HD_B, KV_B * HD_B
    rows = sb * dl
    return pl.pallas_call(
        functools.partial(_even_sample_body, sb=sb, dl=dl, w_a=w_a),
        grid=(db // sb,),
        in_specs=[pl.BlockSpec((rows, zin), lambda i: (i, 0)),
                  pl.BlockSpec((sb, CONV_W - 1, w_a), lambda i: (i, 0, 0)),
                  pl.BlockSpec((sb, w_a), lambda i: (i, 0)),
                  pl.BlockSpec((sb, WINDOW, nkv), lambda i: (i, 0, 0)),
                  pl.BlockSpec((sb, WINDOW, nkv), lambda i: (i, 0, 0))]
                 + [_resident(a.shape) for a in (cw, cb, wri, bri, lam, qg, kg, bias, sink)],
        out_specs=[pl.BlockSpec((rows, w_a + nq), lambda i: (i, 0)),
                   pl.BlockSpec((rows, w_a), lambda i: (i, 0)),
                   pl.BlockSpec((sb, WINDOW, nkv), lambda i: (i, 0, 0)),
                   pl.BlockSpec((sb, WINDOW, nkv), lambda i: (i, 0, 0))],
        out_shape=[jax.ShapeDtypeStruct((n, w_a + nq), BF16),
                   jax.ShapeDtypeStruct((n, w_a), F32),
                   jax.ShapeDtypeStruct((db, WINDOW, nkv), F32),
                   jax.ShapeDtypeStruct((db, WINDOW, nkv), F32)],
        scratch_shapes=[pltpu.VMEM((sb, 2 * SUBLANES, w_a), F32)],
        compiler_params=_params(), name="even_sample")(
            z, conv, h0, ck, cv, cw, cb, wri, bri, lam, qg, kg, bias, sink)


def _retention_log_decay():
    return [float(np.log1p(-np.exp2(np.float32(-5.0 - h)))) for h in range(H_C)]


def _rotate(x, cosf, sinf, axis):
    return x * cosf + pltpu.roll(x, x.shape[axis] // 2, axis=axis) * sinf


ODD_TILE_STAGES = 2 * H_C + 2 * H_D + 2


def _odd_tile(z_ref, y_ref, cosf, sinf, side, dmask_ref, cw_ref, cb_ref, gb_ref, s_s, c_s, n_s, m_s, qkbuf, *,
              tile):
    wq = H_C * DK_C
    wv = H_C * DV_C
    o_kc, o_vc, o_gc = wq, 2 * wq, 2 * wq + wv
    o_qk = 2 * wq + 2 * wv
    wqk = 2 * H_D * DK_D
    o_vd = o_qk + wqk
    o_od = o_vd + H_D * DV_D
    o_gt = o_od + H_D * DV_D
    lg = _retention_log_decay()
    ti = lax.broadcasted_iota(jnp.int32, (tile, 1), 0).astype(F32)

    for h in range(H_C):
        q = _rotate(z_ref[:, h * DK_C:(h + 1) * DK_C], cosf, sinf, 1)
        k = _rotate(z_ref[:, o_kc + h * DK_C:o_kc + (h + 1) * DK_C], cosf, sinf, 1) * (DK_C ** -0.5)
        v = z_ref[:, o_vc + h * DV_C:o_vc + (h + 1) * DV_C]
        xi = jnp.exp((ti + 1.0) * lg[h])
        zeta = jnp.exp((tile - 1.0 - ti) * lg[h])
        sc = _dot_nt(q, k) * dmask_ref[h]
        o = _dot(sc, v) + _dot(q * xi, s_s[h])
        side.tick()
        s_s[h] = math.exp(tile * lg[h]) * s_s[h] + _dot_tn(k * zeta, v)
        gate = z_ref[:, o_gc + h * DV_C:o_gc + (h + 1) * DV_C]
        y_ref[:, h * DV_C:(h + 1) * DV_C] = (o * _rms_scale(o) * _silu(gate)).astype(BF16)
        side.tick()

    xqk = z_ref[:, o_qk:o_qk + wqk]
    qkbuf[SUBLANES:SUBLANES + tile, :] = xqk
    cw = cw_ref[...]
    qk = cb_ref[...] + cw[3:4] * xqk
    for i in range(CONV_W - 1):
        off = SUBLANES - (CONV_W - 1) + i
        qk = qk + cw[i:i + 1] * qkbuf[off:off + tile, :]
    qkbuf[0:SUBLANES, :] = qkbuf[tile:tile + SUBLANES, :]
    qk = _silu(qk)
    side.tick()

    gates = z_ref[:, o_gt:o_gt + LANES] + gb_ref[...]
    logf = _log_sigmoid(gates)
    row = lax.broadcasted_iota(jnp.int32, gates.shape, 0)
    bsum = logf
    s = 1
    while s < tile:
        bsum = bsum + jnp.where(row >= s, pltpu.roll(bsum, s, axis=0), 0.0)
        s *= 2
    bsum_t = bsum.T
    gates_t = gates.T
    ii = lax.broadcasted_iota(jnp.int32, (tile, tile), 0)
    jj = lax.broadcasted_iota(jnp.int32, (tile, tile), 1)
    causal = jj <= ii
    side.tick()
    for h in range(H_D):
        q = qk[:, h * DK_D:(h + 1) * DK_D]
        k = qk[:, H_D * DK_D + h * DK_D:H_D * DK_D + (h + 1) * DK_D] * (DK_D ** -0.5)
        v = z_ref[:, o_vd + h * DV_D:o_vd + (h + 1) * DV_D]
        b_col = bsum[:, H_D + h:H_D + h + 1]
        b_row = bsum_t[H_D + h:H_D + h + 1, :]
        i_col = gates[:, h:h + 1]
        i_row = gates_t[h:h + 1, :]
        m_prev = m_s[h:h + 1, 0:1]
        dlog = jnp.where(causal, b_col - b_row + i_row, NEG)
        init_log = b_col + m_prev
        m_t = jnp.maximum(init_log, jnp.max(dlog, axis=-1, keepdims=True))
        w = jnp.exp(dlog - m_t)
        a0 = jnp.exp(init_log - m_t)
        sc = _dot_nt(q, k) * w
        num = _dot(sc, v) + a0 * _dot(q, c_s[h])
        den = jnp.sum(sc, axis=-1, keepdims=True) + a0 * jnp.sum(q * n_s[h:h + 1, :], axis=-1, keepdims=True)
        den = jnp.maximum(jnp.abs(den), jnp.exp(-m_t))
        hd = num / den
        og = z_ref[:, o_od + h * DV_D:o_od + (h + 1) * DV_D]
        y_ref[:, wv + h * DV_D:wv + (h + 1) * DV_D] = (hd * _sigmoid(og)).astype(BF16)
        side.tick()
        b_end = b_col[tile - 1:tile, :]
        log_end_col = b_end - b_col + i_col
        m_new = jnp.maximum(b_end + m_prev, jnp.max(log_end_col, axis=0, keepdims=True))
        w_end = jnp.exp(log_end_col - m_new)
        a_end = jnp.exp(b_end + m_prev - m_new)
        kw = k * w_end
        c_s[h] = a_end * c_s[h] + _dot_tn(kw, v)
        n_s[h:h + 1, :] = a_end * n_s[h:h + 1, :] + jnp.sum(kw, axis=0, keepdims=True)
        m_s[h:h + 1, :] = jnp.broadcast_to(m_new, (1, LANES))
        side.tick()


def _odd_prompt_body(x_ref, xn_ref, g_ref, win_ref, wout_ref, rbase_ref, roff_ref, dmask_ref, cw_ref, cb_ref,
                     gb_ref, o_ref, conv_ref, so_ref, co_ref, no_ref, mo_ref,
                     zb0, zb1, yb0, yb1, nbuf, s_s, c_s, n_s, m_s, qkbuf, *, tile):
    step = pl.program_id(0)

    @pl.when(step == 0)
    def _():
        s_s[...] = jnp.zeros_like(s_s)
        c_s[...] = jnp.zeros_like(c_s)
        n_s[...] = jnp.zeros_like(n_s)
        m_s[...] = jnp.zeros_like(m_s)
        qkbuf[0:SUBLANES, :] = jnp.zeros((SUBLANES, qkbuf.shape[1]), F32)
        _SideWork(_proj_pieces(x_ref, 0, tile, g_ref, nbuf, win_ref, zb0, 2 * PROJ_PIECE)).flush()

    mix = functools.partial(_odd_tile, dmask_ref=dmask_ref, cw_ref=cw_ref, cb_ref=cb_ref, gb_ref=gb_ref,
                            s_s=s_s, c_s=c_s, n_s=n_s, m_s=m_s, qkbuf=qkbuf, tile=tile)
    side = _SideWork(_proj_pieces(x_ref, tile, 2 * tile, g_ref, nbuf, win_ref, zb1, 2 * PROJ_PIECE),
                     ODD_TILE_STAGES)

    def rope(t):
        cb, sb = rbase_ref[0, 2 * t:2 * t + 1, :], rbase_ref[0, 2 * t + 1:2 * t + 2, :]
        return (cb * roff_ref[0] - sb * roff_ref[1], sb * roff_ref[2] + cb * roff_ref[3])

    mix(zb0, yb0, *rope(0), side)
    side.flush()
    side = _SideWork(_proj_pieces(xn_ref, 0, tile, g_ref, nbuf, win_ref, zb0, 2 * PROJ_PIECE)
                     + _out_pieces(yb0, wout_ref, x_ref, 0, o_ref, 0, tile, PROJ_PIECE), ODD_TILE_STAGES)
    mix(zb1, yb1, *rope(1), side)
    side.flush()
    _SideWork(_out_pieces(yb1, wout_ref, x_ref, tile, o_ref, tile, tile, PROJ_PIECE)).flush()
    conv_ref[...] = qkbuf[0:SUBLANES, :]
    so_ref[...] = s_s[...]
    co_ref[...] = c_s[...]
    no_ref[...] = n_s[...]
    mo_ref[...] = m_s[...]


def _odd_prompt(x, g, w_in, w_out, rbase, roff, dmask, cw, cb, gb):
    n, d = x.shape
    zin = w_in.shape[1]
    tile = dmask.shape[1]
    assert n % (2 * tile) == 0
    nt = n // tile
    wy = H_C * DV_C + H_D * DV_D
    wqk = 2 * H_D * DK_D
    small = (dmask, cw, cb, gb)
    return pl.pallas_call(
        functools.partial(_odd_prompt_body, tile=tile),
        grid=(nt // 2,),
        in_specs=[pl.BlockSpec((2 * tile, d), lambda i: (i, 0)),
                  pl.BlockSpec((tile, d), lambda i: (jnp.minimum(2 * i + 2, nt - 1), 0)),
                  _resident(g.shape), _resident(w_in.shape), _resident(w_out.shape),
                  pl.BlockSpec((1, 4, DK_C), lambda i: (i, 0, 0)), _resident(roff.shape)]
                 + [_resident(a.shape) for a in small],
        out_specs=[pl.BlockSpec((2 * tile, d), lambda i: (i, 0)),
                   pl.BlockSpec((SUBLANES, wqk), lambda i: (0, 0)),
                   pl.BlockSpec((H_C, DK_C, DV_C), lambda i: (0, 0, 0)),
                   pl.BlockSpec((H_D, DK_D, DV_D), lambda i: (0, 0, 0)),
                   pl.BlockSpec((SUBLANES, DK_D), lambda i: (0, 0)),
                   pl.BlockSpec((SUBLANES, LANES), lambda i: (0, 0))],
        out_shape=[jax.ShapeDtypeStruct((n, d), F32),
                   jax.ShapeDtypeStruct((SUBLANES, wqk), F32),
                   jax.ShapeDtypeStruct((H_C, DK_C, DV_C), F32),
                   jax.ShapeDtypeStruct((H_D, DK_D, DV_D), F32),
                   jax.ShapeDtypeStruct((SUBLANES, DK_D), F32),
                   jax.ShapeDtypeStruct((SUBLANES, LANES), F32)],
        scratch_shapes=[pltpu.VMEM((tile, zin), F32), pltpu.VMEM((tile, zin), F32),
                        pltpu.VMEM((tile, wy), BF16), pltpu.VMEM((tile, wy), BF16),
                        pltpu.VMEM((tile, d), BF16),
                        pltpu.VMEM((H_C, DK_C, DV_C), F32), pltpu.VMEM((H_D, DK_D, DV_D), F32),
                        pltpu.VMEM((SUBLANES, DK_D), F32), pltpu.VMEM((SUBLANES, LANES), F32),
                        pltpu.VMEM((tile + SUBLANES, wqk), F32)],
        compiler_params=_params(), name="odd_prompt")(x, x, g, w_in, w_out, rbase, roff, *small)


def _odd_sample_body(z_ref, cos_ref, sin_ref, s_ref, conv_ref, c_ref, n_ref, m_ref, cw_ref, cb_ref, gb_ref,
                     y_ref, so_ref, co_ref, no_ref, mo_ref, xc, *, sb, dl):
    wq = H_C * DK_C
    wv = H_C * DV_C
    o_kc, o_vc, o_gc = wq, 2 * wq, 2 * wq + wv
    o_qk = 2 * wq + 2 * wv
    wqk = 2 * H_D * DK_D
    o_vd = o_qk + wqk
    o_od = o_vd + H_D * DV_D
    o_gt = o_od + H_D * DV_D
    lg = _retention_log_decay()
    rows = sb * dl

    def z3(c0, width):
        return z_ref[:, c0:c0 + width].reshape(sb, dl, width)

    cosf = cos_ref[...][None]
    sinf = sin_ref[...][None]
    t1 = lax.broadcasted_iota(jnp.int32, (1, dl, 1), 1)
    tf = t1.astype(F32)

    for h in range(H_C):
        q = _rotate(z3(h * DK_C, DK_C), cosf, sinf, 2)
        k = _rotate(z3(o_kc + h * DK_C, DK_C), cosf, sinf, 2) * (DK_C ** -0.5)
        v = z3(o_vc + h * DV_C, DV_C)
        xi = jnp.exp((tf + 1.0) * lg[h])
        zeta = jnp.exp((dl - 1.0 - tf) * lg[h])
        s0 = s_ref[:, h]
        o = jnp.einsum('bqd,bde->bqe', (q * xi).astype(BF16), s0.astype(BF16), preferred_element_type=F32)
        for s in range(dl):
            ks = k if s == 0 else pltpu.roll(k, s, axis=1)
            vs = v if s == 0 else pltpu.roll(v, s, axis=1)
            coef = jnp.sum(q * ks, axis=-1, keepdims=True) * math.exp(s * lg[h])
            o = o + jnp.where(t1 >= s, coef, 0.0) * vs
        upd = jnp.einsum('btd,bte->bde', (k * zeta).astype(BF16), v.astype(BF16), preferred_element_type=F32)
        so_ref[:, h] = math.exp(dl * lg[h]) * s0 + upd
        gate = z3(o_gc + h * DV_C, DV_C)
        y_ref[:, h * DV_C:(h + 1) * DV_C] = (o * _rms_scale(o) * _silu(gate)).reshape(rows, DV_C).astype(BF16)

    xqk3 = z3(o_qk, wqk)
    xc[:, SUBLANES:SUBLANES + dl, :] = xqk3
    xc[:, SUBLANES - (CONV_W - 1):SUBLANES, :] = conv_ref[...]
    cw = cw_ref[...]
    qk = cb_ref[...] + cw[3:4] * xqk3
    for i in range(CONV_W - 1):
        off = SUBLANES - (CONV_W - 1) + i
        qk = qk + cw[i:i + 1] * xc[:, off:off + dl, :]
    qk = _silu(qk)

    gates = z3(o_gt, LANES) + gb_ref[...]
    logf = _log_sigmoid(gates)
    t = lax.broadcasted_iota(jnp.int32, gates.shape, 1)
    bsum = logf
    s = 1
    while s < dl:
        bsum = bsum + jnp.where(t >= s, pltpu.roll(bsum, s, axis=1), 0.0)
        s *= 2
    bsum = pltpu.roll(bsum, LANES - H_D, axis=2)
    m0 = m_ref[...]
    init_log = bsum + m0
    m_t = init_log
    dlogs = []
    for s in range(dl):
        if s == 0:
            d = gates
        else:
            d = bsum - pltpu.roll(bsum, s, axis=1) + pltpu.roll(gates, s, axis=1)
        d = jnp.where(t >= s, d, NEG)
        dlogs.append(d)
        m_t = jnp.maximum(m_t, d)
    a0 = jnp.exp(init_log - m_t)
    ws = [jnp.exp(d - m_t) for d in dlogs]
    inv_floor = jnp.exp(-m_t)
    b_end = bsum[:, dl - 1:dl, :]
    log_end = b_end - bsum + gates
    m_new = jnp.maximum(b_end + m0, jnp.max(log_end, axis=1, keepdims=True))
    w_end = jnp.exp(log_end - m_new)
    a_end = jnp.exp(b_end + m0 - m_new)
    mo_ref[...] = m_new
    for h in range(H_D):
        q = qk[:, :, h * DK_D:(h + 1) * DK_D]
        k = qk[:, :, H_D * DK_D + h * DK_D:H_D * DK_D + (h + 1) * DK_D] * (DK_D ** -0.5)
        v = z3(o_vd + h * DV_D, DV_D)
        c0 = c_ref[:, h]
        n0 = n_ref[:, h:h + 1, :]
        a0h = a0[:, :, h:h + 1]
        num = a0h * jnp.einsum('bqd,bde->bqe', q.astype(BF16), c0.astype(BF16), preferred_element_type=F32)
        den = a0h * jnp.sum(q * n0, axis=-1, keepdims=True)
        for s in range(dl):
            ks = k if s == 0 else pltpu.roll(k, s, axis=1)
            vs = v if s == 0 else pltpu.roll(v, s, axis=1)
            coef = jnp.sum(q * ks, axis=-1, keepdims=True) * ws[s][:, :, h:h + 1]
            num = num + coef * vs
            den = den + coef
        den = jnp.maximum(jnp.abs(den), inv_floor[:, :, h:h + 1])
        og = z3(o_od + h * DV_D, DV_D)
        y_ref[:, wv + h * DV_D:wv + (h + 1) * DV_D] = (
            (num / den) * _sigmoid(og)).reshape(rows, DV_D).astype(BF16)
        kw = k * w_end[:, :, h:h + 1]
        aeh = a_end[:, :, h:h + 1]
        upd = jnp.einsum('btd,bte->bde', kw.astype(BF16), v.astype(BF16), preferred_element_type=F32)
        co_ref[:, h] = aeh * c0 + upd
        no_ref[:, h:h + 1, :] = aeh * n0 + jnp.sum(kw, axis=1, keepdims=True)


def _odd_sample(z, cosf, sinf, s0, conv, c0, n0, m0, cw, cb, gb, *, dl):
    n, zin = z.shape
    db = n // dl
    sb = min(ODD_SB, db)
    assert db % sb == 0 and dl == SUBLANES
    rows = sb * dl
    wy = H_C * DV_C + H_D * DV_D
    wqk = 2 * H_D * DK_D
    st_spec = pl.BlockSpec((sb, H_C, DK_C, DV_C), lambda i: (i, 0, 0, 0))
    n_spec = pl.BlockSpec((sb, H_D, DK_D), lambda i: (i, 0, 0))
    m_spec = pl.BlockSpec((sb, 1, LANES), lambda i: (i, 0, 0))
    return pl.pallas_call(
        functools.partial(_odd_sample_body, sb=sb, dl=dl),
        grid=(db // sb,),
        in_specs=[pl.BlockSpec((rows, zin), lambda i: (i, 0)), _resident(cosf.shape), _resident(sinf.shape),
                  st_spec, pl.BlockSpec((sb, CONV_W - 1, wqk), lambda i: (i, 0, 0)), st_spec, n_spec, m_spec]
                 + [_resident(a.shape) for a in (cw, cb, gb)],
        out_specs=[pl.BlockSpec((rows, wy), lambda i: (i, 0)), st_spec, st_spec, n_spec, m_spec],
        out_shape=[jax.ShapeDtypeStruct((n, wy), BF16),
                   jax.ShapeDtypeStruct(s0.shape, F32), jax.ShapeDtypeStruct(c0.shape, F32),
                   jax.ShapeDtypeStruct(n0.shape, F32), jax.ShapeDtypeStruct(m0.shape, F32)],
        scratch_shapes=[pltpu.VMEM((sb, 2 * SUBLANES, wqk), F32)],
        compiler_params=_params(), name="odd_sample")(z, cosf, sinf, s0, conv, c0, n0, m0, cw, cb, gb)


def _t5_bucket(dist):
    n = np.maximum(dist, 0)
    max_exact = N_BUCKETS // 2
    large = max_exact + (np.log(np.maximum(n, max_exact) / max_exact)
                         / math.log(MAX_DIST / max_exact) * (N_BUCKETS - max_exact)).astype(np.int32)
    return np.where(n < max_exact, n, np.minimum(large, N_BUCKETS - 1)).astype(np.int32)


def _attn_bias(rel_bias, nq_rows, nkeys, p0):
    rel = np.arange(nq_rows)[:, None] + WINDOW - np.arange(nkeys)[None, :]
    kpos_ok = (p0 - WINDOW + np.arange(nkeys)) >= 0
    mask = (rel >= 0) & (rel < WINDOW) & kpos_ok[None, :]
    onehot = jnp.asarray(_t5_bucket(rel)[:, :, None] == np.arange(N_BUCKETS), F32)
    bias = jnp.einsum('qkb,bh->hqk', onehot, rel_bias.astype(F32), precision=lax.Precision.HIGHEST)
    bias = jnp.where(mask[None], bias, NEG)
    return bias.reshape(KV_B, G_B * nq_rows, nkeys)


def _sink_rows(sinks, nq_rows, width=1):
    col = jnp.repeat(sinks.astype(F32).reshape(KV_B, G_B), nq_rows, axis=1).reshape(KV_B, G_B * nq_rows, 1)
    return jnp.broadcast_to(col, (KV_B, G_B * nq_rows, width))


def _rope_tables(p0, n):
    half = DK_C // 2
    pos = p0 + jnp.arange(n, dtype=F32)
    inv = ROPE_BASE ** (-jnp.arange(half, dtype=F32) / half)
    ang = pos[:, None] * inv[None, :]
    cos, sin = jnp.cos(ang), jnp.sin(ang)
    return jnp.concatenate([cos, cos], axis=1), jnp.concatenate([-sin, sin], axis=1)


def _rope_split_tables(n, tile):
    half = DK_C // 2
    inv = ROPE_BASE ** (-jnp.arange(half, dtype=F32) / half)
    dup = lambda a: jnp.concatenate([a, a], axis=-1)
    base = (jnp.arange(n // tile, dtype=F32) * tile)[:, None] * inv[None, :]
    rbase = jnp.stack([dup(jnp.cos(base)), dup(jnp.sin(base))], axis=1).reshape(n // (2 * tile), 4, DK_C)
    off = jnp.arange(tile, dtype=F32)[:, None] * inv[None, :]
    sign = jnp.concatenate([-jnp.ones((half,), F32), jnp.ones((half,), F32)])
    co, so = dup(jnp.cos(off)), dup(jnp.sin(off))
    return rbase, jnp.stack([co, so, co * sign, so * sign])


def _decay_mask(tile):
    t = np.arange(tile)
    diff = (t[:, None] - t[None, :]).astype(np.float32)
    lg = np.asarray(_retention_log_decay(), np.float32)
    return jnp.asarray(np.where(diff >= 0, np.exp(np.maximum(diff, 0)[None] * lg[:, None, None]), 0.0), F32)


def _block_diag_gates(w_r, w_i, group):
    nb, c, _ = w_r.shape
    eye = jnp.eye(group, dtype=w_r.dtype)

    def bd(w):
        w4 = w.reshape(nb // group, group, c, c)
        return jnp.einsum('gncd,nm->gncmd', w4, eye).reshape(nb // group, group * c, group * c)

    return jnp.concatenate([bd(w_r), bd(w_i)], axis=2).astype(BF16)


def kernel(x_prompt, x_sample, state_a_conv, state_a_h, cache_b_k, cache_b_v, state_c_S, state_d_conv, state_d_C, state_d_n, state_d_m, norm_g, ffn1_w_in, ffn1_w_out, ffn2_w_in, ffn2_w_out, even_w_in, even_w_out, a_conv_w, a_conv_b, a_w_r, a_b_r, a_w_i, a_b_i, a_lambda, b_qk_norm, b_sinks, rel_bias, odd_w_in, odd_w_out, d_conv_w, d_conv_b, d_gate_b):
    bp, lp, d = x_prompt.shape
    db, dl, _ = x_sample.shape
    assert bp == 1, "prompt group is a single sequence"
    depth = norm_g.shape[0]
    w_a = a_conv_w.shape[2]
    nkv = KV_B * HD_B
    wqk = 2 * H_D * DK_D
    tile = min(MIX_TILE, lp // 2)

    xp = x_prompt.reshape(lp, d)
    xs = x_sample.reshape(db * dl, d)

    bias_p = _attn_bias(rel_bias, WINDOW, 2 * WINDOW, WINDOW)
    bias_s = _attn_bias(rel_bias, dl, WINDOW + dl, PAST_LEN)
    rbase_p, roff_p = _rope_split_tables(lp, tile)
    cos_s, sin_s = _rope_tables(float(PAST_LEN), dl)
    dmask = _decay_mask(tile)

    st = {}
    for li in range(depth):
        j = li // 2
        xs, wg, wu, wo = _ffn_cast(xs, norm_g[li, 0], ffn1_w_in, ffn1_w_out, li)
        if li % 2 == 1:
            xp = _ffn(xp, norm_g[li, 0], wg, wu, wo)
        if li % 2 == 0:
            zs, w_in = _norm_proj_cast(xs, norm_g[li, 1], even_w_in, j, EVEN_IN_CHUNK)
            shared = (a_conv_w[j], a_conv_b[j].reshape(1, w_a),
                      _block_diag_gates(a_w_r[j], a_w_i[j], 4),
                      jnp.stack([a_b_r[j], a_b_i[j]]), a_lambda[j].reshape(1, w_a),
                      b_qk_norm[j, 0].reshape(1, HD_B), jnp.tile(b_qk_norm[j, 1], KV_B).reshape(1, nkv))
            ys, hs_all, k_s, v_s = _even_sample(
                zs, state_a_conv[j], state_a_h[j], cache_b_k[j].reshape(db, WINDOW, nkv),
                cache_b_v[j].reshape(db, WINDOW, nkv), *shared, bias_s, _sink_rows(b_sinks[j], dl), dl=dl)
            xs, w_out = _out_proj_cast(ys, even_w_out, j, xs)
            seg = jnp.kron(jnp.eye(H_B, dtype=F32), jnp.ones((HD_B, HD_B), F32)).astype(BF16)
            xp, conv_p, h_p, k_p, v_p = _even_prompt(
                xp, norm_g[li, 0].reshape(1, d), wg, wu, wo,
                norm_g[li, 1].reshape(1, d), w_in, w_out, *shared[:5],
                jnp.tile(b_qk_norm[j, 0], H_B).reshape(1, H_B * HD_B), shared[6], seg,
                bias_p, _sink_rows(b_sinks[j], WINDOW, LANES))
            st.setdefault('a_conv', ([], []))
            st['a_conv'][0].append(conv_p[SUBLANES - (CONV_W - 1):].reshape(1, CONV_W - 1, w_a))
            st['a_conv'][1].append(zs.reshape(db, dl, -1)[:, dl - (CONV_W - 1):, :w_a])
            st.setdefault('a_h', ([], []))
            st['a_h'][0].append(h_p[0:1])
            st['a_h'][1].append(hs_all.reshape(db, dl, w_a)[:, dl - 1])
            st.setdefault('b_k', ([], []))
            st['b_k'][0].append(k_p.reshape(1, WINDOW, KV_B, HD_B))
            st['b_k'][1].append(k_s.reshape(db, WINDOW, KV_B, HD_B))
            st.setdefault('b_v', ([], []))
            st['b_v'][0].append(v_p.reshape(1, WINDOW, KV_B, HD_B))
            st['b_v'][1].append(v_s.reshape(db, WINDOW, KV_B, HD_B))
        else:
            zs, w_in = _norm_proj_cast(xs, norm_g[li, 1], odd_w_in, j, ODD_IN_CHUNK)
            o_qk = 2 * H_C * DK_C + 2 * H_C * DV_C
            gb = jnp.pad(d_gate_b[j], (0, LANES - 2 * H_D)).reshape(1, LANES)
            shared = (d_conv_w[j], d_conv_b[j].reshape(1, wqk), gb)
            m0 = jnp.pad(state_d_m[j], ((0, 0), (0, LANES - H_D))).reshape(db, 1, LANES)
            ys, s_s, c_s, n_s, m_s = _odd_sample(zs, cos_s, sin_s, state_c_S[j], state_d_conv[j],
                                                 state_d_C[j], state_d_n[j], m0, *shared, dl=dl)
            xs, w_out = _out_proj_cast(ys, odd_w_out, j, xs)
            xp, dconv_p, s_p, c_p, n_p, m_p = _odd_prompt(xp, norm_g[li, 1].reshape(1, d), w_in, w_out,
                                                          rbase_p, roff_p, dmask, *shared)
            st.setdefault('c_S', ([], []))
            st['c_S'][0].append(s_p[None])
            st['c_S'][1].append(s_s)
            st.setdefault('d_conv', ([], []))
            st['d_conv'][0].append(dconv_p[SUBLANES - (CONV_W - 1):].reshape(1, CONV_W - 1, wqk))
            st['d_conv'][1].append(zs.reshape(db, dl, -1)[:, dl - (CONV_W - 1):, o_qk:o_qk + wqk])
            st.setdefault('d_C', ([], []))
            st['d_C'][0].append(c_p[None])
            st['d_C'][1].append(c_s)
            st.setdefault('d_n', ([], []))
            st['d_n'][0].append(n_p[None, :H_D])
            st['d_n'][1].append(n_s)
            st.setdefault('d_m', ([], []))
            st['d_m'][0].append(m_p[:H_D, 0].reshape(1, H_D))
            st['d_m'][1].append(m_s[:, 0, :H_D])
        xs, wg, wu, wo = _ffn_cast(xs, norm_g[li, 2], ffn2_w_in, ffn2_w_out, li)
        xp = _ffn(xp, norm_g[li, 2], wg, wu, wo)

    outs = [xp.reshape(1, lp, d), xs.reshape(db, dl, d)]
    for name in ('a_conv', 'a_h', 'b_k', 'b_v', 'c_S', 'd_conv', 'd_C', 'd_n', 'd_m'):
        outs.append(jnp.stack(st[name][0]))
        outs.append(jnp.stack(st[name][1]))
    return tuple(outs)
```

```python
import functools
import math

import jax
import jax.numpy as jnp
import numpy as np
from jax import lax
from jax.experimental import pallas as pl
from jax.experimental.pallas import tpu as pltpu

F32 = jnp.float32
BF16 = jnp.bfloat16

PAST_LEN = 16384
EPS = 1e-6
CONV_W = 4
C_A = 8.0
NB_A = 16
H_B, KV_B, HD_B = 8, 2, 64
G_B = H_B // KV_B
WINDOW = 128
N_BUCKETS = 32
MAX_DIST = 128
H_C, DK_C, DV_C = 4, 128, 256
H_D, DK_D, DV_D = 4, 128, 256
ROPE_BASE = 10000.0
NEG = -1e30

LANES = 128
SUBLANES = 8
VMEM_LIMIT = 56 * 1024 * 1024
VMEM_LIMIT_MAX = 60000 * 1024

FFN_TILE = 1024
MIX_TILE = 256
FF_CHUNK = 256
EVEN_SB = 32
ODD_SB = 8
PROJ_PIECE = 256
EVEN_IN_CHUNK = 1408
ODD_IN_CHUNK = 896


def _params(n_grid_dims=1, vmem_limit=VMEM_LIMIT):
    return pltpu.CompilerParams(dimension_semantics=("arbitrary",) * n_grid_dims,
                                vmem_limit_bytes=vmem_limit)


def _resident(shape):
    nd = len(shape)
    return pl.BlockSpec(shape, lambda i, _nd=nd: (0,) * _nd, pipeline_mode=pl.Buffered(1))


def _dot(a, b):
    return jnp.dot(a.astype(BF16), b.astype(BF16), preferred_element_type=F32)


def _dot_nt(a, b):
    return lax.dot_general(a.astype(BF16), b.astype(BF16), (((1,), (1,)), ((), ())),
                           preferred_element_type=F32)


def _dot_tn(a, b):
    return lax.dot_general(a.astype(BF16), b.astype(BF16), (((0,), (0,)), ((), ())),
                           preferred_element_type=F32)


def _rms_scale(x):
    return lax.rsqrt(jnp.mean(x * x, axis=-1, keepdims=True) + EPS)


def _sigmoid(x):
    return 1.0 / (1.0 + jnp.exp(-x))


def _silu(x):
    return x * _sigmoid(x)


def _log_sigmoid(x):
    return jnp.minimum(x, 0.0) - jnp.log1p(jnp.exp(-jnp.abs(x)))


def _gelu_tanh(x):
    k = 2.0 * math.sqrt(2.0 / math.pi)
    return x / (1.0 + jnp.exp(x * (-k - (k * 0.044715) * (x * x))))


def _sqrt_nonneg(x):
    return jnp.where(x > 0.0, x * lax.rsqrt(x), 0.0)


def _ffn_body(x_ref, g_ref, wg_ref, wu_ref, wo_ref, o_ref, *, d_ff, chunk):
    x = x_ref[...]
    nb = (x * _rms_scale(x) * g_ref[...]).astype(BF16)
    acc = None
    for c0 in range(0, d_ff, chunk):
        gate = jnp.dot(nb, wg_ref[:, c0:c0 + chunk], preferred_element_type=F32)
        up = jnp.dot(nb, wu_ref[:, c0:c0 + chunk], preferred_element_type=F32)
        mid = (_silu(gate) * up).astype(BF16)
        part = jnp.dot(mid, wo_ref[c0:c0 + chunk, :], preferred_element_type=F32)
        acc = part if acc is None else acc + part
    o_ref[...] = x + 0.5 * acc


def _ffn(x, g, wg, wu, wo):
    n, d = x.shape
    d_ff = wo.shape[0]
    tm = min(FFN_TILE, n)
    assert n % tm == 0 and d_ff % FF_CHUNK == 0
    return pl.pallas_call(
        functools.partial(_ffn_body, d_ff=d_ff, chunk=FF_CHUNK),
        grid=(n // tm,),
        in_specs=[pl.BlockSpec((tm, d), lambda i: (i, 0)), _resident((1, d)),
                  _resident(wg.shape), _resident(wu.shape), _resident(wo.shape)],
        out_specs=pl.BlockSpec((tm, d), lambda i: (i, 0)),
        out_shape=jax.ShapeDtypeStruct((n, d), F32),
        compiler_params=_params(), name="ffn")(x, g.reshape(1, d), wg, wu, wo)


def _ffn_cast_body(x_ref, g_ref, wg_ref, wu_ref, wo_ref, o_ref, wgb_ref, wub_ref, wob_ref, nb_ref, acc_ref):
    j = pl.program_id(0)

    @pl.when(j == 0)
    def _():
        x = x_ref[...]
        nb_ref[...] = (x * _rms_scale(x) * g_ref[...]).astype(BF16)
        acc_ref[...] = jnp.zeros_like(acc_ref)

    wg = wg_ref[...].astype(BF16)
    wu = wu_ref[...].astype(BF16)
    wo = wo_ref[...].astype(BF16)
    wgb_ref[...] = wg
    wub_ref[...] = wu
    wob_ref[...] = wo
    nb = nb_ref[...]
    gate = jnp.dot(nb, wg, preferred_element_type=F32)
    up = jnp.dot(nb, wu, preferred_element_type=F32)
    acc_ref[...] += jnp.dot((_silu(gate) * up).astype(BF16), wo, preferred_element_type=F32)

    @pl.when(j == pl.num_programs(0) - 1)
    def _():
        o_ref[...] = x_ref[...] + 0.5 * acc_ref[...]


def _ffn_cast(x, g, w_in, w_out, li):
    n, d = x.shape
    d_ff = w_out.shape[1]
    nc = d_ff // FF_CHUNK
    assert d_ff % FF_CHUNK == 0
    return pl.pallas_call(
        _ffn_cast_body, grid=(nc,),
        in_specs=[pl.BlockSpec((n, d), lambda j: (0, 0)), pl.BlockSpec((1, d), lambda j: (0, 0)),
                  pl.BlockSpec((None, d, FF_CHUNK), lambda j: (li, 0, j)),
                  pl.BlockSpec((None, d, FF_CHUNK), lambda j: (li, 0, j + nc)),
                  pl.BlockSpec((None, FF_CHUNK, d), lambda j: (li, j, 0))],
        out_specs=[pl.BlockSpec((n, d), lambda j: (0, 0)),
                   pl.BlockSpec((d, FF_CHUNK), lambda j: (0, j)),
                   pl.BlockSpec((d, FF_CHUNK), lambda j: (0, j)),
                   pl.BlockSpec((FF_CHUNK, d), lambda j: (j, 0))],
        out_shape=[jax.ShapeDtypeStruct((n, d), F32), jax.ShapeDtypeStruct((d, d_ff), BF16),
                   jax.ShapeDtypeStruct((d, d_ff), BF16), jax.ShapeDtypeStruct((d_ff, d), BF16)],
        scratch_shapes=[pltpu.VMEM((n, d), BF16), pltpu.VMEM((n, d), F32)],
        compiler_params=_params(), name="ffn_cast")(x, g.reshape(1, d), w_in, w_in, w_out)


def _norm_proj_cast_body(x_ref, g_ref, w_ref, z_ref, wb_ref, nb_ref, *, chunk, valid):
    j = pl.program_id(0)

    @pl.when(j == 0)
    def _():
        x = x_ref[...]
        nb_ref[...] = (x * _rms_scale(x) * g_ref[...]).astype(BF16)

    col = j * chunk + lax.broadcasted_iota(jnp.int32, w_ref.shape, 1)
    w = jnp.where(col < valid, w_ref[...], 0.0).astype(BF16)
    wb_ref[...] = w
    z_ref[...] = jnp.dot(nb_ref[...], w, preferred_element_type=F32)


def _norm_proj_cast(x, g, w_all, li, chunk):
    n, d = x.shape
    m = w_all.shape[2]
    nc = -(-m // chunk)
    mp = nc * chunk
    return pl.pallas_call(
        functools.partial(_norm_proj_cast_body, chunk=chunk, valid=m), grid=(nc,),
        in_specs=[pl.BlockSpec((n, d), lambda j: (0, 0)), pl.BlockSpec((1, d), lambda j: (0, 0)),
                  pl.BlockSpec((None, d, chunk), lambda j: (li, 0, j))],
        out_specs=[pl.BlockSpec((n, chunk), lambda j: (0, j)), pl.BlockSpec((d, chunk), lambda j: (0, j))],
        out_shape=[jax.ShapeDtypeStruct((n, mp), F32), jax.ShapeDtypeStruct((d, mp), BF16)],
        scratch_shapes=[pltpu.VMEM((n, d), BF16)],
        compiler_params=_params(), name="norm_proj_cast")(x, g.reshape(1, d), w_all)


def _out_proj_cast_body(y_ref, w_ref, x_ref, o_ref, wb_ref):
    w = w_ref[...].astype(BF16)
    wb_ref[...] = w
    o_ref[...] = x_ref[...] + jnp.dot(y_ref[...], w, preferred_element_type=F32)


def _out_proj_cast(y, w_all, li, x):
    n, k = y.shape
    d = w_all.shape[2]
    chunk = PROJ_PIECE
    assert d % chunk == 0
    return pl.pallas_call(
        _out_proj_cast_body, grid=(d // chunk,),
        in_specs=[pl.BlockSpec((n, k), lambda j: (0, 0)),
                  pl.BlockSpec((None, k, chunk), lambda j: (li, 0, j)),
                  pl.BlockSpec((n, chunk), lambda j: (0, j))],
        out_specs=[pl.BlockSpec((n, chunk), lambda j: (0, j)), pl.BlockSpec((k, chunk), lambda j: (0, j))],
        out_shape=[jax.ShapeDtypeStruct((n, d), F32), jax.ShapeDtypeStruct((k, d), BF16)],
        compiler_params=_params(), name="out_proj_cast")(y, w_all, x)


class _SideWork:
    def __init__(self, pieces=(), stages=1):
        self.pieces, self.stages, self.done, self.stage = list(pieces), stages, 0, 0

    def tick(self):
        self.stage += 1
        upto = min(len(self.pieces), -(-len(self.pieces) * self.stage // self.stages))
        while self.done < upto:
            self.pieces[self.done]()
            self.done += 1

    def flush(self):
        self.stage = self.stages - 1
        self.tick()


def _norm_piece(x_ref, r0, r1, g_ref, nb_ref):
    x = x_ref[r0:r1, :]
    nb_ref[...] = (x * _rms_scale(x) * g_ref[...]).astype(BF16)


def _proj_piece(nb_ref, w_ref, z_ref, c0, c1):
    z_ref[:, c0:c1] = jnp.dot(nb_ref[...], w_ref[:, c0:c1], preferred_element_type=F32)


def _proj_pieces(x_ref, r0, r1, g_ref, nb_ref, w_ref, z_ref, width):
    m = w_ref.shape[1]
    return [functools.partial(_norm_piece, x_ref, r0, r1, g_ref, nb_ref)] + [
        functools.partial(_proj_piece, nb_ref, w_ref, z_ref, c0, min(c0 + width, m))
        for c0 in range(0, m, width)]


def _out_piece(y_ref, y_val, w_ref, x_ref, xr0, o_ref, or0, rows, c0, c1):
    if not y_val:
        y_val.append(y_ref[...])
    o_ref[or0:or0 + rows, c0:c1] = x_ref[xr0:xr0 + rows, c0:c1] + jnp.dot(
        y_val[0], w_ref[:, c0:c1], preferred_element_type=F32)


def _out_pieces(y_ref, w_ref, x_ref, xr0, o_ref, or0, rows, width):
    m = w_ref.shape[1]
    y_val = []
    return [functools.partial(_out_piece, y_ref, y_val, w_ref, x_ref, xr0, o_ref, or0, rows,
                              c0, min(c0 + width, m)) for c0 in range(0, m, width)]


def _ffn_up_piece(nb_ref, wg_ref, wu_ref, mid_ref, c0, c1):
    nb = nb_ref[...]
    gate = jnp.dot(nb, wg_ref[:, c0:c1], preferred_element_type=F32)
    up = jnp.dot(nb, wu_ref[:, c0:c1], preferred_element_type=F32)
    mid_ref[...] = (_silu(gate) * up).astype(BF16)


def _ffn_down_piece(mid_ref, wo_ref, acc_ref, c0, c1):
    part = jnp.dot(mid_ref[...], wo_ref[c0:c1, :], preferred_element_type=F32)
    if c0 == 0:
        acc_ref[...] = part
    else:
        acc_ref[...] += part


def _ffn_final_piece(x_ref, r0, r1, acc_ref, o_ref):
    o_ref[...] = x_ref[r0:r1, :] + 0.5 * acc_ref[...]


def _ffn_pieces(x_ref, r0, r1, g_ref, nb_ref, wg_ref, wu_ref, wo_ref, mid_refs, acc_ref, o_ref, width):
    d_ff = wo_ref.shape[0]
    assert d_ff % width == 0
    ups = [functools.partial(_ffn_up_piece, nb_ref, wg_ref, wu_ref, mid_refs[(c0 // width) % 2], c0, c0 + width)
           for c0 in range(0, d_ff, width)]
    downs = [functools.partial(_ffn_down_piece, mid_refs[(c0 // width) % 2], wo_ref, acc_ref, c0, c0 + width)
             for c0 in range(0, d_ff, width)]
    order = [ups[0]]
    for c in range(1, len(ups)):
        order += [ups[c], downs[c - 1]]
    order.append(downs[-1])
    return ([functools.partial(_norm_piece, x_ref, r0, r1, g_ref, nb_ref)] + order
            + [functools.partial(_ffn_final_piece, x_ref, r0, r1, acc_ref, o_ref)])


def _rglru_gate_groups(y, wri_ref, bri_ref, lam_ref):
    w = y.shape[1]
    gw = wri_ref.shape[1]
    yb = y.astype(BF16)
    logsig = _log_sigmoid(lam_ref[...])
    for g in range(w // gw):
        cols = slice(g * gw, (g + 1) * gw)
        ri = jnp.dot(yb[:, cols], wri_ref[g], preferred_element_type=F32)
        r = _sigmoid(ri[:, :gw] + bri_ref[0:1, cols])
        i = _sigmoid(ri[:, gw:] + bri_ref[1:2, cols])
        log_a = C_A * r * logsig[:, cols]
        th = jnp.tanh(log_a)
        one_minus_a2 = -2.0 * th / (1.0 - th)
        yield jnp.exp(log_a), _sqrt_nonneg(one_minus_a2) * (i * y[:, cols])


def _rglru_gates(y, wri_ref, bri_ref, lam_ref):
    parts = list(_rglru_gate_groups(y, wri_ref, bri_ref, lam_ref))
    return jnp.concatenate([p[0] for p in parts], axis=1), jnp.concatenate([p[1] for p in parts], axis=1)


def _group_scan(a3, u3):
    t = lax.broadcasted_iota(jnp.int32, a3.shape, 1)
    s = 1
    while s < SUBLANES:
        keep = t >= s
        u3 = jnp.where(keep, a3 * pltpu.roll(u3, s, axis=1) + u3, u3)
        a3 = jnp.where(keep, a3 * pltpu.roll(a3, s, axis=1), a3)
        s *= 2
    return a3, u3


EVEN_TILE_STAGES = 16


def _even_rglru_branch(z_ref, y_ref, cw_ref, cb_ref, wri_ref, bri_ref, lam_ref, conv_ref, h_ref,
                       xbuf, hcar, hbuf, *, tile, w_a):
    xa = z_ref[:, 0:w_a]
    xbuf[SUBLANES:SUBLANES + tile, :] = xa
    cw = cw_ref[...]
    y = cb_ref[...] + cw[3:4] * xa
    for i in range(CONV_W - 1):
        off = SUBLANES - (CONV_W - 1) + i
        y = y + cw[i:i + 1] * xbuf[off:off + tile, :]
    xbuf[0:SUBLANES, :] = xbuf[tile:tile + SUBLANES, :]
    conv_ref[...] = xbuf[0:SUBLANES, :]
    yield

    a_parts, u_parts = [], []
    for a_g, u_g in _rglru_gate_groups(y, wri_ref, bri_ref, lam_ref):
        a_parts.append(a_g)
        u_parts.append(u_g)
        yield
    a = jnp.concatenate(a_parts, axis=1)
    u = jnp.concatenate(u_parts, axis=1)

    ng = tile // SUBLANES
    a3, u3 = _group_scan(a.reshape(ng, SUBLANES, w_a), u.reshape(ng, SUBLANES, w_a))
    yield
    carry = hcar[0:1, :]
    for g in range(ng):
        hg = u3[g] + a3[g] * carry
        hbuf[g * SUBLANES:(g + 1) * SUBLANES, :] = hg
        carry = hg[SUBLANES - 1:SUBLANES, :]
    hcar[0:1, :] = carry
    h_ref[...] = jnp.broadcast_to(carry, h_ref.shape)
    yield
    y_ref[:, 0:w_a] = (hbuf[...] * _gelu_tanh(z_ref[:, w_a:2 * w_a])).astype(BF16)
    yield


def _head_mean_sq(x):
    x2 = x * x
    lo = lax.broadcasted_iota(jnp.int32, (x.shape[0], LANES), 1) < HD_B
    groups = []
    for v in range(x.shape[1] // LANES):
        blk = x2[:, v * LANES:(v + 1) * LANES]
        s_lo = jnp.sum(jnp.where(lo, blk, 0.0), axis=-1, keepdims=True)
        s_hi = jnp.sum(jnp.where(lo, 0.0, blk), axis=-1, keepdims=True)
        groups.append(jnp.where(lo, s_lo, s_hi))
    return jnp.concatenate(groups, axis=1) * (1.0 / HD_B)


def _even_attn_branch(z_ref, y_ref, is_first, qg_ref, kg_ref, bias_ref, sink_ref, kl_ref, vl_ref,
                      kbuf, vbuf, *, tile, w_a):
    nq = H_B * HD_B
    nkv = KV_B * HD_B
    k = z_ref[:, 2 * w_a + nq:2 * w_a + nq + nkv]
    kbuf[WINDOW:WINDOW + tile, :] = k * lax.rsqrt(_head_mean_sq(k) + EPS) * kg_ref[...]
    vbuf[WINDOW:WINDOW + tile, :] = z_ref[:, 2 * w_a + nq + nkv:2 * w_a + nq + 2 * nkv]
    yield

    chains = [(nb, kv) for nb in range(tile // WINDOW) for kv in range(KV_B)]
    rows = G_B * WINDOW
    q_all = z_ref[:, 2 * w_a:2 * w_a + nq]
    q_all = q_all * lax.rsqrt(_head_mean_sq(q_all) + EPS) * qg_ref[...] * (HD_B ** -0.5)
    qs = jnp.concatenate([q_all[nb * WINDOW:(nb + 1) * WINDOW, (kv * G_B + g) * HD_B:(kv * G_B + g + 1) * HD_B]
                          for nb, kv in chains for g in range(G_B)], axis=0).astype(BF16)
    yield
    col = lax.broadcasted_iota(jnp.int32, (rows, 2 * WINDOW), 1)
    s_parts = []
    for c, (nb, kv) in enumerate(chains):
        kk = kbuf[nb * WINDOW:(nb + 2) * WINDOW, kv * HD_B:(kv + 1) * HD_B]
        s = _dot_nt(qs[c * rows:(c + 1) * rows], kk) + bias_ref[kv]
        if nb == 0 and is_first is not False:
            s = jnp.where(jnp.logical_and(is_first, col < WINDOW), NEG, s)
        s_parts.append(s)
        yield
    s = jnp.concatenate(s_parts, axis=0)
    sink = jnp.concatenate([sink_ref[kv] for _, kv in chains], axis=0)
    m = jnp.maximum(jnp.max(s, axis=-1, keepdims=True), sink)
    p = jnp.exp(s - jnp.concatenate([m] * (2 * WINDOW // LANES), axis=1))
    inv = (1.0 / (jnp.sum(p, axis=-1, keepdims=True) + jnp.exp(sink - m)))[:, 0:HD_B]
    p = p.astype(BF16)
    yield
    o = jnp.concatenate([_dot(p[c * rows:(c + 1) * rows],
                              vbuf[nb * WINDOW:(nb + 2) * WINDOW, kv * HD_B:(kv + 1) * HD_B])
                         for c, (nb, kv) in enumerate(chains)], axis=0) * inv
    y_ref[:, w_a:w_a + nq] = jnp.concatenate(
        [jnp.concatenate([o[(c * G_B + g) * WINDOW:(c * G_B + g + 1) * WINDOW, :]
                          for c, (cb_, _) in enumerate(chains) if cb_ == nb for g in range(G_B)], axis=1)
         for nb in range(tile // WINDOW)], axis=0).astype(BF16)

    kbuf[0:WINDOW, :] = kbuf[tile:tile + WINDOW, :]
    vbuf[0:WINDOW, :] = vbuf[tile:tile + WINDOW, :]
    kl_ref[...] = kbuf[0:WINDOW, :]
    vl_ref[...] = vbuf[0:WINDOW, :]
    yield


def _even_tile(z_ref, y_ref, is_first, side, cw_ref, cb_ref, wri_ref, bri_ref, lam_ref, qg_ref, kg_ref,
               bias_ref, sink_ref, conv_ref, h_ref, kl_ref, vl_ref, xbuf, hcar, hbuf, kbuf, vbuf, *, tile, w_a):
    branches = [
        _even_rglru_branch(z_ref, y_ref, cw_ref, cb_ref, wri_ref, bri_ref, lam_ref, conv_ref, h_ref,
                           xbuf, hcar, hbuf, tile=tile, w_a=w_a),
        _even_attn_branch(z_ref, y_ref, is_first, qg_ref, kg_ref, bias_ref, sink_ref, kl_ref, vl_ref,
                          kbuf, vbuf, tile=tile, w_a=w_a)]
    while branches:
        for b in list(branches):
            if next(b, StopIteration) is StopIteration:
                branches.remove(b)
            else:
                side.tick()


def _even_prompt_body(x_ref, xn_ref, gf_ref, wg_ref, wu_ref, wo_ref, g_ref, win_ref, wout_ref, cw_ref, cb_ref,
                      wri_ref, bri_ref, lam_ref, qg_ref, kg_ref, bias_ref, sink_ref,
                      o_ref, conv_ref, h_ref, kl_ref, vl_ref,
                      zb0, zb1, yb0, yb1, x1b0, x1b1, acc, mid0, mid1, nbuf, xbuf, hcar, hbuf, kbuf, vbuf, *,
                      tile, w_a):
    step = pl.program_id(0)
    nkv = KV_B * HD_B

    def ffn_proj(src_ref, r0, x1b, zb):
        return (_ffn_pieces(src_ref, r0, r0 + tile, gf_ref, nbuf, wg_ref, wu_ref, wo_ref, (mid0, mid1), acc, x1b,
                            FF_CHUNK)
                + _proj_pieces(x1b, 0, tile, g_ref, nbuf, win_ref, zb, PROJ_PIECE))

    @pl.when(step == 0)
    def _():
        xbuf[0:SUBLANES, :] = jnp.zeros((SUBLANES, w_a), F32)
        hcar[...] = jnp.zeros_like(hcar)
        kbuf[0:WINDOW, :] = jnp.zeros((WINDOW, nkv), F32)
        vbuf[0:WINDOW, :] = jnp.zeros((WINDOW, nkv), F32)
        _SideWork(ffn_proj(x_ref, 0, x1b0, zb0)).flush()

    mix = functools.partial(_even_tile, cw_ref=cw_ref, cb_ref=cb_ref, wri_ref=wri_ref, bri_ref=bri_ref,
                            lam_ref=lam_ref, qg_ref=qg_ref, kg_ref=kg_ref, bias_ref=bias_ref,
                            sink_ref=sink_ref, conv_ref=conv_ref, h_ref=h_ref, kl_ref=kl_ref, vl_ref=vl_ref,
                            xbuf=xbuf, hcar=hcar, hbuf=hbuf, kbuf=kbuf, vbuf=vbuf, tile=tile, w_a=w_a)
    side = _SideWork(ffn_proj(x_ref, tile, x1b1, zb1), EVEN_TILE_STAGES)
    mix(zb0, yb0, step == 0, side)
    side.flush()
    side = _SideWork(_out_pieces(yb0, wout_ref, x1b0, 0, o_ref, 0, tile, PROJ_PIECE)
                     + ffn_proj(xn_ref, 0, x1b0, zb0), EVEN_TILE_STAGES)
    mix(zb1, yb1, False, side)
    side.flush()
    _SideWork(_out_pieces(yb1, wout_ref, x1b1, 0, o_ref, tile, tile, PROJ_PIECE)).flush()


def _even_prompt(x, gf, wg, wu, wo, g, w_in, w_out, cw, cb, wri, bri, lam, qg, kg, bias, sink):
    n, d = x.shape
    zin = w_in.shape[1]
    w_a = cw.shape[1]
    tile = min(MIX_TILE, n // 2)
    assert n % (2 * tile) == 0 and tile % WINDOW == 0
    nt = n // tile
    nq, nkv = H_B * HD_B, KV_B * HD_B
    small = (gf, wg, wu, wo, g, w_in, w_out, cw, cb, wri, bri, lam, qg, kg, bias, sink)
    return pl.pallas_call(
        functools.partial(_even_prompt_body, tile=tile, w_a=w_a),
        grid=(nt // 2,),
        in_specs=[pl.BlockSpec((2 * tile, d), lambda i: (i, 0)),
                  pl.BlockSpec((tile, d), lambda i: (jnp.minimum(2 * i + 2, nt - 1), 0))]
                 + [_resident(a.shape) for a in small],
        out_specs=[pl.BlockSpec((2 * tile, d), lambda i: (i, 0)),
                   pl.BlockSpec((SUBLANES, w_a), lambda i: (0, 0)),
                   pl.BlockSpec((SUBLANES, w_a), lambda i: (0, 0)),
                   pl.BlockSpec((WINDOW, nkv), lambda i: (0, 0)),
                   pl.BlockSpec((WINDOW, nkv), lambda i: (0, 0))],
        out_shape=[jax.ShapeDtypeStruct((n, d), F32),
                   jax.ShapeDtypeStruct((SUBLANES, w_a), F32),
                   jax.ShapeDtypeStruct((SUBLANES, w_a), F32),
                   jax.ShapeDtypeStruct((WINDOW, nkv), F32),
                   jax.ShapeDtypeStruct((WINDOW, nkv), F32)],
        scratch_shapes=[pltpu.VMEM((tile, zin), F32), pltpu.VMEM((tile, zin), F32),
                        pltpu.VMEM((tile, w_a + nq), BF16), pltpu.VMEM((tile, w_a + nq), BF16),
                        pltpu.VMEM((tile, d), F32), pltpu.VMEM((tile, d), F32), pltpu.VMEM((tile, d), F32),
                        pltpu.VMEM((tile, FF_CHUNK), BF16), pltpu.VMEM((tile, FF_CHUNK), BF16),
                        pltpu.VMEM((tile, d), BF16),
                        pltpu.VMEM((tile + SUBLANES, w_a), F32), pltpu.VMEM((SUBLANES, w_a), F32),
                        pltpu.VMEM((tile, w_a), F32),
                        pltpu.VMEM((tile + WINDOW, nkv), F32), pltpu.VMEM((tile + WINDOW, nkv), F32)],
        compiler_params=_params(vmem_limit=VMEM_LIMIT_MAX), name="even_prompt")(x, x, *small)


def _even_sample_body(z_ref, conv_ref, h0_ref, ck_ref, cv_ref, cw_ref, cb_ref, wri_ref, bri_ref, lam_ref,
                      qg_ref, kg_ref, bias_ref, sink_ref, y_ref, co_ref, hl_ref, ko_ref, vo_ref, xc, *, sb, dl, w_a):
    nq = H_B * HD_B
    nkv = KV_B * HD_B
    rows = sb * dl
    nkeys = WINDOW + dl

    xa3 = z_ref[:, 0:w_a].reshape(sb, dl, w_a)
    xc[:, SUBLANES:SUBLANES + dl, :] = xa3
    xc[:, SUBLANES - (CONV_W - 1):SUBLANES, :] = conv_ref[...]
    cw = cw_ref[...]
    y3 = cb_ref[...] + cw[3:4] * xa3
    for i in range(CONV_W - 1):
        off = SUBLANES - (CONV_W - 1) + i
        y3 = y3 + cw[i:i + 1] * xc[:, off:off + dl, :]
    y = y3.reshape(rows, w_a)

    a, u = _rglru_gates(y, wri_ref, bri_ref, lam_ref)
    a3 = a.reshape(sb, dl, w_a)
    u3 = u.reshape(sb, dl, w_a)
    t = lax.broadcasted_iota(jnp.int32, a3.shape, 1)
    h0 = jnp.broadcast_to(h0_ref[...][:, None, :], a3.shape)
    u3 = jnp.where(t == 0, u3 + a3 * h0, u3)
    _, h3 = _group_scan(a3, u3)
    hs = h3.reshape(rows, w_a)
    hl_ref[...] = h3[:, dl - 1:dl, :]
    co_ref[...] = xa3[:, dl - (CONV_W - 1):dl, :]
    y_ref[:, 0:w_a] = (hs * _gelu_tanh(z_ref[:, w_a:2 * w_a])).astype(BF16)

    q3 = z_ref[:, 2 * w_a:2 * w_a + nq].reshape(sb, dl, nq)
    k3 = z_ref[:, 2 * w_a + nq:2 * w_a + nq + nkv].reshape(sb, dl, nkv)
    v3 = z_ref[:, 2 * w_a + nq + nkv:2 * w_a + nq + 2 * nkv].reshape(sb, dl, nkv)
    kparts = []
    for hh in range(KV_B):
        kh = k3[:, :, hh * HD_B:(hh + 1) * HD_B]
        kparts.append(kh * _rms_scale(kh))
    kn3 = jnp.concatenate(kparts, axis=2) * kg_ref[...]
    ko_ref[:, 0:WINDOW - dl, :] = ck_ref[:, dl:WINDOW, :]
    ko_ref[:, WINDOW - dl:WINDOW, :] = kn3
    vo_ref[:, 0:WINDOW - dl, :] = cv_ref[:, dl:WINDOW, :]
    vo_ref[:, WINDOW - dl:WINDOW, :] = v3

    outs = []
    for kv in range(KV_B):
        hs_ = slice(kv * HD_B, (kv + 1) * HD_B)
        qs = jnp.concatenate([q3[:, :, (kv * G_B + g) * HD_B:(kv * G_B + g + 1) * HD_B]
                              for g in range(G_B)], axis=1)
        qs = qs * _rms_scale(qs) * qg_ref[...] * (HD_B ** -0.5)
        kc = jnp.concatenate([ck_ref[:, :, hs_], kn3[:, :, hs_]], axis=1)
        vc = jnp.concatenate([cv_ref[:, :, hs_], v3[:, :, hs_]], axis=1)
        s = jnp.einsum('bqd,bkd->bqk', qs.astype(BF16), kc.astype(BF16),
                       preferred_element_type=F32) + bias_ref[kv]
        sink = sink_ref[kv]
        m = jnp.maximum(jnp.max(s, axis=-1, keepdims=True), sink)
        p = jnp.exp(s - m)
        den = jnp.sum(p, axis=-1, keepdims=True) + jnp.exp(sink - m)
        o = jnp.einsum('bqk,bkd->bqd', p.astype(BF16), vc.astype(BF16),
                       preferred_element_type=F32) / den
        outs += [o[:, g * dl:(g + 1) * dl, :] for g in range(G_B)]
    y_ref[:, w_a:w_a + nq] = jnp.concatenate(outs, axis=2).reshape(rows, nq).astype(BF16)


def _even_sample(z, conv, h0, ck, cv, cw, cb, wri, bri, lam, qg, kg, bias, sink, *, dl):
    n, zin = z.shape
    db = n // dl
    w_a = cw.shape[1]
    sb = min(EVEN_SB, db)
    assert db % sb == 0 and dl == SUBLANES
    nq, nkv = H_B * HD_B, KV_B * HD_B
    rows = sb * dl
    return pl.pallas_call(
        functools.partial(_even_sample_body, sb=sb, dl=dl, w_a=w_a),
        grid=(db // sb,),
        in_specs=[pl.BlockSpec((rows, zin), lambda i: (i, 0)),
                  pl.BlockSpec((sb, CONV_W - 1, w_a), lambda i: (i, 0, 0)),
                  pl.BlockSpec((sb, w_a), lambda i: (i, 0)),
                  pl.BlockSpec((sb, WINDOW, nkv), lambda i: (i, 0, 0)),
                  pl.BlockSpec((sb, WINDOW, nkv), lambda i: (i, 0, 0))]
                 + [_resident(a.shape) for a in (cw, cb, wri, bri, lam, qg, kg, bias, sink)],
        out_specs=[pl.BlockSpec((rows, w_a + nq), lambda i: (i, 0)),
                   pl.BlockSpec((sb, CONV_W - 1, w_a), lambda i: (i, 0, 0)),
                   pl.BlockSpec((sb, 1, w_a), lambda i: (i, 0, 0)),
                   pl.BlockSpec((sb, WINDOW, nkv), lambda i: (i, 0, 0)),
                   pl.BlockSpec((sb, WINDOW, nkv), lambda i: (i, 0, 0))],
        out_shape=[jax.ShapeDtypeStruct((n, w_a + nq), BF16),
                   jax.ShapeDtypeStruct((db, CONV_W - 1, w_a), F32),
                   jax.ShapeDtypeStruct((db, 1, w_a), F32),
                   jax.ShapeDtypeStruct((db, WINDOW, nkv), F32),
                   jax.ShapeDtypeStruct((db, WINDOW, nkv), F32)],
        scratch_shapes=[pltpu.VMEM((sb, 2 * SUBLANES, w_a), F32)],
        compiler_params=_params(), name="even_sample")(
            z, conv, h0, ck, cv, cw, cb, wri, bri, lam, qg, kg, bias, sink)


def _retention_log_decay():
    return [float(np.log1p(-np.exp2(np.float32(-5.0 - h)))) for h in range(H_C)]


def _rotate(x, cosf, sinf, axis):
    return x * cosf + pltpu.roll(x, x.shape[axis] // 2, axis=axis) * sinf


ODD_TILE_STAGES = 2 * H_C + 2 * H_D + 2


def _odd_tile(z_ref, y_ref, cosf, sinf, side, dmask_ref, cw_ref, cb_ref, gb_ref, s_s, c_s, n_s, m_s, qkbuf, *,
              tile):
    wq = H_C * DK_C
    wv = H_C * DV_C
    o_kc, o_vc, o_gc = wq, 2 * wq, 2 * wq + wv
    o_qk = 2 * wq + 2 * wv
    wqk = 2 * H_D * DK_D
    o_vd = o_qk + wqk
    o_od = o_vd + H_D * DV_D
    o_gt = o_od + H_D * DV_D
    lg = _retention_log_decay()
    ti = lax.broadcasted_iota(jnp.int32, (tile, 1), 0).astype(F32)

    for h in range(H_C):
        q = _rotate(z_ref[:, h * DK_C:(h + 1) * DK_C], cosf, sinf, 1)
        k = _rotate(z_ref[:, o_kc + h * DK_C:o_kc + (h + 1) * DK_C], cosf, sinf, 1) * (DK_C ** -0.5)
        v = z_ref[:, o_vc + h * DV_C:o_vc + (h + 1) * DV_C]
        xi = jnp.exp((ti + 1.0) * lg[h])
        zeta = jnp.exp((tile - 1.0 - ti) * lg[h])
        sc = _dot_nt(q, k) * dmask_ref[h]
        o = _dot(sc, v) + _dot(q * xi, s_s[h])
        side.tick()
        s_s[h] = math.exp(tile * lg[h]) * s_s[h] + _dot_tn(k * zeta, v)
        gate = z_ref[:, o_gc + h * DV_C:o_gc + (h + 1) * DV_C]
        y_ref[:, h * DV_C:(h + 1) * DV_C] = (o * _rms_scale(o) * _silu(gate)).astype(BF16)
        side.tick()

    xqk = z_ref[:, o_qk:o_qk + wqk]
    qkbuf[SUBLANES:SUBLANES + tile, :] = xqk
    cw = cw_ref[...]
    qk = cb_ref[...] + cw[3:4] * xqk
    for i in range(CONV_W - 1):
        off = SUBLANES - (CONV_W - 1) + i
        qk = qk + cw[i:i + 1] * qkbuf[off:off + tile, :]
    qkbuf[0:SUBLANES, :] = qkbuf[tile:tile + SUBLANES, :]
    qk = _silu(qk)
    side.tick()

    gates = z_ref[:, o_gt:o_gt + LANES] + gb_ref[...]
    logf = _log_sigmoid(gates)
    row = lax.broadcasted_iota(jnp.int32, gates.shape, 0)
    bsum = logf
    s = 1
    while s < tile:
        bsum = bsum + jnp.where(row >= s, pltpu.roll(bsum, s, axis=0), 0.0)
        s *= 2
    bsum_t = bsum.T
    gates_t = gates.T
    ii = lax.broadcasted_iota(jnp.int32, (tile, tile), 0)
    jj = lax.broadcasted_iota(jnp.int32, (tile, tile), 1)
    causal = jj <= ii
    side.tick()
    for h in range(H_D):
        q = qk[:, h * DK_D:(h + 1) * DK_D]
        k = qk[:, H_D * DK_D + h * DK_D:H_D * DK_D + (h + 1) * DK_D] * (DK_D ** -0.5)
        v = z_ref[:, o_vd + h * DV_D:o_vd + (h + 1) * DV_D]
        b_col = bsum[:, H_D + h:H_D + h + 1]
        b_row = bsum_t[H_D + h:H_D + h + 1, :]
        i_col = gates[:, h:h + 1]
        i_row = gates_t[h:h + 1, :]
        m_prev = m_s[h:h + 1, 0:1]
        dlog = jnp.where(causal, b_col - b_row + i_row, NEG)
        init_log = b_col + m_prev
        m_t = jnp.maximum(init_log, jnp.max(dlog, axis=-1, keepdims=True))
        w = jnp.exp(dlog - m_t)
        a0 = jnp.exp(init_log - m_t)
        sc = _dot_nt(q, k) * w
        num = _dot(sc, v) + a0 * _dot(q, c_s[h])
        den = jnp.sum(sc, axis=-1, keepdims=True) + a0 * jnp.sum(q * n_s[h:h + 1, :], axis=-1, keepdims=True)
        den = jnp.maximum(jnp.abs(den), jnp.exp(-m_t))
        hd = num / den
        og = z_ref[:, o_od + h * DV_D:o_od + (h + 1) * DV_D]
        y_ref[:, wv + h * DV_D:wv + (h + 1) * DV_D] = (hd * _sigmoid(og)).astype(BF16)
        side.tick()
        b_end = b_col[tile - 1:tile, :]
        log_end_col = b_end - b_col + i_col
        m_new = jnp.maximum(b_end + m_prev, jnp.max(log_end_col, axis=0, keepdims=True))
        w_end = jnp.exp(log_end_col - m_new)
        a_end = jnp.exp(b_end + m_prev - m_new)
        kw = k * w_end
        c_s[h] = a_end * c_s[h] + _dot_tn(kw, v)
        n_s[h:h + 1, :] = a_end * n_s[h:h + 1, :] + jnp.sum(kw, axis=0, keepdims=True)
        m_s[h:h + 1, :] = jnp.broadcast_to(m_new, (1, LANES))
        side.tick()


def _odd_prompt_body(x_ref, xn_ref, g_ref, win_ref, wout_ref, rbase_ref, roff_ref, dmask_ref, cw_ref, cb_ref,
                     gb_ref, o_ref, conv_ref, so_ref, co_ref, no_ref, mo_ref,
                     zb0, zb1, yb0, yb1, nbuf, s_s, c_s, n_s, m_s, qkbuf, *, tile):
    step = pl.program_id(0)

    @pl.when(step == 0)
    def _():
        s_s[...] = jnp.zeros_like(s_s)
        c_s[...] = jnp.zeros_like(c_s)
        n_s[...] = jnp.zeros_like(n_s)
        m_s[...] = jnp.zeros_like(m_s)
        qkbuf[0:SUBLANES, :] = jnp.zeros((SUBLANES, qkbuf.shape[1]), F32)
        _SideWork(_proj_pieces(x_ref, 0, tile, g_ref, nbuf, win_ref, zb0, 2 * PROJ_PIECE)).flush()

    mix = functools.partial(_odd_tile, dmask_ref=dmask_ref, cw_ref=cw_ref, cb_ref=cb_ref, gb_ref=gb_ref,
                            s_s=s_s, c_s=c_s, n_s=n_s, m_s=m_s, qkbuf=qkbuf, tile=tile)
    side = _SideWork(_proj_pieces(x_ref, tile, 2 * tile, g_ref, nbuf, win_ref, zb1, 2 * PROJ_PIECE),
                     ODD_TILE_STAGES)

    def rope(t):
        cb, sb = rbase_ref[0, 2 * t:2 * t + 1, :], rbase_ref[0, 2 * t + 1:2 * t + 2, :]
        return (cb * roff_ref[0] - sb * roff_ref[1], sb * roff_ref[2] + cb * roff_ref[3])

    mix(zb0, yb0, *rope(0), side)
    side.flush()
    side = _SideWork(_proj_pieces(xn_ref, 0, tile, g_ref, nbuf, win_ref, zb0, 2 * PROJ_PIECE)
                     + _out_pieces(yb0, wout_ref, x_ref, 0, o_ref, 0, tile, PROJ_PIECE), ODD_TILE_STAGES)
    mix(zb1, yb1, *rope(1), side)
    side.flush()
    _SideWork(_out_pieces(yb1, wout_ref, x_ref, tile, o_ref, tile, tile, PROJ_PIECE)).flush()
    conv_ref[...] = qkbuf[0:SUBLANES, :]
    so_ref[...] = s_s[...]
    co_ref[...] = c_s[...]
    no_ref[...] = n_s[...]
    mo_ref[...] = m_s[...]


def _odd_prompt(x, g, w_in, w_out, rbase, roff, dmask, cw, cb, gb):
    n, d = x.shape
    zin = w_in.shape[1]
    tile = dmask.shape[1]
    assert n % (2 * tile) == 0
    nt = n // tile
    wy = H_C * DV_C + H_D * DV_D
    wqk = 2 * H_D * DK_D
    small = (dmask, cw, cb, gb)
    return pl.pallas_call(
        functools.partial(_odd_prompt_body, tile=tile),
        grid=(nt // 2,),
        in_specs=[pl.BlockSpec((2 * tile, d), lambda i: (i, 0)),
                  pl.BlockSpec((tile, d), lambda i: (jnp.minimum(2 * i + 2, nt - 1), 0)),
                  _resident(g.shape), _resident(w_in.shape), _resident(w_out.shape),
                  pl.BlockSpec((1, 4, DK_C), lambda i: (i, 0, 0)), _resident(roff.shape)]
                 + [_resident(a.shape) for a in small],
        out_specs=[pl.BlockSpec((2 * tile, d), lambda i: (i, 0)),
                   pl.BlockSpec((SUBLANES, wqk), lambda i: (0, 0)),
                   pl.BlockSpec((H_C, DK_C, DV_C), lambda i: (0, 0, 0)),
                   pl.BlockSpec((H_D, DK_D, DV_D), lambda i: (0, 0, 0)),
                   pl.BlockSpec((SUBLANES, DK_D), lambda i: (0, 0)),
                   pl.BlockSpec((SUBLANES, LANES), lambda i: (0, 0))],
        out_shape=[jax.ShapeDtypeStruct((n, d), F32),
                   jax.ShapeDtypeStruct((SUBLANES, wqk), F32),
                   jax.ShapeDtypeStruct((H_C, DK_C, DV_C), F32),
                   jax.ShapeDtypeStruct((H_D, DK_D, DV_D), F32),
                   jax.ShapeDtypeStruct((SUBLANES, DK_D), F32),
                   jax.ShapeDtypeStruct((SUBLANES, LANES), F32)],
        scratch_shapes=[pltpu.VMEM((tile, zin), F32), pltpu.VMEM((tile, zin), F32),
                        pltpu.VMEM((tile, wy), BF16), pltpu.VMEM((tile, wy), BF16),
                        pltpu.VMEM((tile, d), BF16),
                        pltpu.VMEM((H_C, DK_C, DV_C), F32), pltpu.VMEM((H_D, DK_D, DV_D), F32),
                        pltpu.VMEM((SUBLANES, DK_D), F32), pltpu.VMEM((SUBLANES, LANES), F32),
                        pltpu.VMEM((tile + SUBLANES, wqk), F32)],
        compiler_params=_params(), name="odd_prompt")(x, x, g, w_in, w_out, rbase, roff, *small)


def _odd_sample_body(z_ref, cos_ref, sin_ref, s_ref, conv_ref, c_ref, n_ref, m_ref, cw_ref, cb_ref, gb_ref,
                     y_ref, so_ref, cvo_ref, co_ref, no_ref, mo_ref, xc, *, sb, dl):
    wq = H_C * DK_C
    wv = H_C * DV_C
    o_kc, o_vc, o_gc = wq, 2 * wq, 2 * wq + wv
    o_qk = 2 * wq + 2 * wv
    wqk = 2 * H_D * DK_D
    o_vd = o_qk + wqk
    o_od = o_vd + H_D * DV_D
    o_gt = o_od + H_D * DV_D
    lg = _retention_log_decay()
    rows = sb * dl

    def z3(c0, width):
        return z_ref[:, c0:c0 + width].reshape(sb, dl, width)

    cosf = cos_ref[...][None]
    sinf = sin_ref[...][None]
    t1 = lax.broadcasted_iota(jnp.int32, (1, dl, 1), 1)
    tf = t1.astype(F32)

    for h in range(H_C):
        q = _rotate(z3(h * DK_C, DK_C), cosf, sinf, 2)
        k = _rotate(z3(o_kc + h * DK_C, DK_C), cosf, sinf, 2) * (DK_C ** -0.5)
        v = z3(o_vc + h * DV_C, DV_C)
        xi = jnp.exp((tf + 1.0) * lg[h])
        zeta = jnp.exp((dl - 1.0 - tf) * lg[h])
        s0 = s_ref[:, h]
        o = jnp.einsum('bqd,bde->bqe', (q * xi).astype(BF16), s0.astype(BF16), preferred_element_type=F32)
        for s in range(dl):
            ks = k if s == 0 else pltpu.roll(k, s, axis=1)
            vs = v if s == 0 else pltpu.roll(v, s, axis=1)
            coef = jnp.sum(q * ks, axis=-1, keepdims=True) * math.exp(s * lg[h])
            o = o + jnp.where(t1 >= s, coef, 0.0) * vs
        upd = jnp.einsum('btd,bte->bde', (k * zeta).astype(BF16), v.astype(BF16), preferred_element_type=F32)
        so_ref[:, h] = math.exp(dl * lg[h]) * s0 + upd
        gate = z3(o_gc + h * DV_C, DV_C)
        y_ref[:, h * DV_C:(h + 1) * DV_C] = (o * _rms_scale(o) * _silu(gate)).reshape(rows, DV_C).astype(BF16)

    xqk3 = z3(o_qk, wqk)
    xc[:, SUBLANES:SUBLANES + dl, :] = xqk3
    xc[:, SUBLANES - (CONV_W - 1):SUBLANES, :] = conv_ref[...]
    cvo_ref[...] = xqk3[:, dl - (CONV_W - 1):dl, :]
    cw = cw_ref[...]
    qk = cb_ref[...] + cw[3:4] * xqk3
    for i in range(CONV_W - 1):
        off = SUBLANES - (CONV_W - 1) + i
        qk = qk + cw[i:i + 1] * xc[:, off:off + dl, :]
    qk = _silu(qk)

    gates = z3(o_gt, LANES) + gb_ref[...]
    logf = _log_sigmoid(gates)
    t = lax.broadcasted_iota(jnp.int32, gates.shape, 1)
    bsum = logf
    s = 1
    while s < dl:
        bsum = bsum + jnp.where(t >= s, pltpu.roll(bsum, s, axis=1), 0.0)
        s *= 2
    bsum = pltpu.roll(bsum, LANES - H_D, axis=2)
    m0 = m_ref[...]
    init_log = bsum + m0
    m_t = init_log
    dlogs = []
    for s in range(dl):
        if s == 0:
            d = gates
        else:
            d = bsum - pltpu.roll(bsum, s, axis=1) + pltpu.roll(gates, s, axis=1)
        d = jnp.where(t >= s, d, NEG)
        dlogs.append(d)
        m_t = jnp.maximum(m_t, d)
    a0 = jnp.exp(init_log - m_t)
    ws = [jnp.exp(d - m_t) for d in dlogs]
    inv_floor = jnp.exp(-m_t)
    b_end = bsum[:, dl - 1:dl, :]
    log_end = b_end - bsum + gates
    m_new = jnp.maximum(b_end + m0, jnp.max(log_end, axis=1, keepdims=True))
    w_end = jnp.exp(log_end - m_new)
    a_end = jnp.exp(b_end + m0 - m_new)
    mo_ref[...] = m_new
    for h in range(H_D):
        q = qk[:, :, h * DK_D:(h + 1) * DK_D]
        k = qk[:, :, H_D * DK_D + h * DK_D:H_D * DK_D + (h + 1) * DK_D] * (DK_D ** -0.5)
        v = z3(o_vd + h * DV_D, DV_D)
        c0 = c_ref[:, h]
        n0 = n_ref[:, h:h + 1, :]
        a0h = a0[:, :, h:h + 1]
        num = a0h * jnp.einsum('bqd,bde->bqe', q.astype(BF16), c0.astype(BF16), preferred_element_type=F32)
        den = a0h * jnp.sum(q * n0, axis=-1, keepdims=True)
        for s in range(dl):
            ks = k if s == 0 else pltpu.roll(k, s, axis=1)
            vs = v if s == 0 else pltpu.roll(v, s, axis=1)
            coef = jnp.sum(q * ks, axis=-1, keepdims=True) * ws[s][:, :, h:h + 1]
            num = num + coef * vs
            den = den + coef
        den = jnp.maximum(jnp.abs(den), inv_floor[:, :, h:h + 1])
        og = z3(o_od + h * DV_D, DV_D)
        y_ref[:, wv + h * DV_D:wv + (h + 1) * DV_D] = (
            (num / den) * _sigmoid(og)).reshape(rows, DV_D).astype(BF16)
        kw = k * w_end[:, :, h:h + 1]
        aeh = a_end[:, :, h:h + 1]
        upd = jnp.einsum('btd,bte->bde', kw.astype(BF16), v.astype(BF16), preferred_element_type=F32)
        co_ref[:, h] = aeh * c0 + upd
        no_ref[:, h:h + 1, :] = aeh * n0 + jnp.sum(kw, axis=1, keepdims=True)


def _odd_sample(z, cosf, sinf, s0, conv, c0, n0, m0, cw, cb, gb, *, dl):
    n, zin = z.shape
    db = n // dl
    sb = min(ODD_SB, db)
    assert db % sb == 0 and dl == SUBLANES
    rows = sb * dl
    wy = H_C * DV_C + H_D * DV_D
    wqk = 2 * H_D * DK_D
    st_spec = pl.BlockSpec((sb, H_C, DK_C, DV_C), lambda i: (i, 0, 0, 0))
    n_spec = pl.BlockSpec((sb, H_D, DK_D), lambda i: (i, 0, 0))
    m_spec = pl.BlockSpec((sb, 1, LANES), lambda i: (i, 0, 0))
    return pl.pallas_call(
        functools.partial(_odd_sample_body, sb=sb, dl=dl),
        grid=(db // sb,),
        in_specs=[pl.BlockSpec((rows, zin), lambda i: (i, 0)), _resident(cosf.shape), _resident(sinf.shape),
                  st_spec, pl.BlockSpec((sb, CONV_W - 1, wqk), lambda i: (i, 0, 0)), st_spec, n_spec, m_spec]
                 + [_resident(a.shape) for a in (cw, cb, gb)],
        out_specs=[pl.BlockSpec((rows, wy), lambda i: (i, 0)), st_spec,
                   pl.BlockSpec((sb, CONV_W - 1, wqk), lambda i: (i, 0, 0)), st_spec, n_spec, m_spec],
        out_shape=[jax.ShapeDtypeStruct((n, wy), BF16),
                   jax.ShapeDtypeStruct(s0.shape, F32), jax.ShapeDtypeStruct(conv.shape, F32),
                   jax.ShapeDtypeStruct(c0.shape, F32),
                   jax.ShapeDtypeStruct(n0.shape, F32), jax.ShapeDtypeStruct(m0.shape, F32)],
        scratch_shapes=[pltpu.VMEM((sb, 2 * SUBLANES, wqk), F32)],
        compiler_params=_params(), name="odd_sample")(z, cosf, sinf, s0, conv, c0, n0, m0, cw, cb, gb)


def _t5_bucket(dist):
    n = np.maximum(dist, 0)
    max_exact = N_BUCKETS // 2
    large = max_exact + (np.log(np.maximum(n, max_exact) / max_exact)
                         / math.log(MAX_DIST / max_exact) * (N_BUCKETS - max_exact)).astype(np.int32)
    return np.where(n < max_exact, n, np.minimum(large, N_BUCKETS - 1)).astype(np.int32)


def _attn_bias(rel_bias, nq_rows, nkeys, p0):
    rel = np.arange(nq_rows)[:, None] + WINDOW - np.arange(nkeys)[None, :]
    kpos_ok = (p0 - WINDOW + np.arange(nkeys)) >= 0
    mask = (rel >= 0) & (rel < WINDOW) & kpos_ok[None, :]
    onehot = jnp.asarray(_t5_bucket(rel)[:, :, None] == np.arange(N_BUCKETS), F32)
    bias = jnp.einsum('qkb,bh->hqk', onehot, rel_bias.astype(F32), precision=lax.Precision.HIGHEST)
    bias = jnp.where(mask[None], bias, NEG)
    return bias.reshape(KV_B, G_B * nq_rows, nkeys)


def _sink_rows(sinks, nq_rows, width=1):
    col = jnp.repeat(sinks.astype(F32).reshape(KV_B, G_B), nq_rows, axis=1).reshape(KV_B, G_B * nq_rows, 1)
    return jnp.broadcast_to(col, (KV_B, G_B * nq_rows, width))


def _rope_tables(p0, n):
    half = DK_C // 2
    pos = p0 + jnp.arange(n, dtype=F32)
    inv = ROPE_BASE ** (-jnp.arange(half, dtype=F32) / half)
    ang = pos[:, None] * inv[None, :]
    cos, sin = jnp.cos(ang), jnp.sin(ang)
    return jnp.concatenate([cos, cos], axis=1), jnp.concatenate([-sin, sin], axis=1)


def _rope_split_tables(n, tile):
    half = DK_C // 2
    inv = ROPE_BASE ** (-jnp.arange(half, dtype=F32) / half)
    dup = lambda a: jnp.concatenate([a, a], axis=-1)
    base = (jnp.arange(n // tile, dtype=F32) * tile)[:, None] * inv[None, :]
    rbase = jnp.stack([dup(jnp.cos(base)), dup(jnp.sin(base))], axis=1).reshape(n // (2 * tile), 4, DK_C)
    off = jnp.arange(tile, dtype=F32)[:, None] * inv[None, :]
    sign = jnp.concatenate([-jnp.ones((half,), F32), jnp.ones((half,), F32)])
    co, so = dup(jnp.cos(off)), dup(jnp.sin(off))
    return rbase, jnp.stack([co, so, co * sign, so * sign])


def _decay_mask(tile):
    t = np.arange(tile)
    diff = (t[:, None] - t[None, :]).astype(np.float32)
    lg = np.asarray(_retention_log_decay(), np.float32)
    return jnp.asarray(np.where(diff >= 0, np.exp(np.maximum(diff, 0)[None] * lg[:, None, None]), 0.0), F32)


def _block_diag_gates(w_r, w_i, group):
    nb, c, _ = w_r.shape
    eye = jnp.eye(group, dtype=w_r.dtype)

    def bd(w):
        w4 = w.reshape(nb // group, group, c, c)
        return jnp.einsum('gncd,nm->gncmd', w4, eye).reshape(nb // group, group * c, group * c)

    return jnp.concatenate([bd(w_r), bd(w_i)], axis=2).astype(BF16)


def kernel(x_prompt, x_sample, state_a_conv, state_a_h, cache_b_k, cache_b_v, state_c_S, state_d_conv, state_d_C, state_d_n, state_d_m, norm_g, ffn1_w_in, ffn1_w_out, ffn2_w_in, ffn2_w_out, even_w_in, even_w_out, a_conv_w, a_conv_b, a_w_r, a_b_r, a_w_i, a_b_i, a_lambda, b_qk_norm, b_sinks, rel_bias, odd_w_in, odd_w_out, d_conv_w, d_conv_b, d_gate_b):
    bp, lp, d = x_prompt.shape
    db, dl, _ = x_sample.shape
    assert bp == 1, "prompt group is a single sequence"
    depth = norm_g.shape[0]
    w_a = a_conv_w.shape[2]
    nkv = KV_B * HD_B
    wqk = 2 * H_D * DK_D
    tile = min(MIX_TILE, lp // 2)

    xp = x_prompt.reshape(lp, d)
    xs = x_sample.reshape(db * dl, d)

    bias_p = _attn_bias(rel_bias, WINDOW, 2 * WINDOW, WINDOW)
    bias_s = _attn_bias(rel_bias, dl, WINDOW + dl, PAST_LEN)
    rbase_p, roff_p = _rope_split_tables(lp, tile)
    cos_s, sin_s = _rope_tables(float(PAST_LEN), dl)
    dmask = _decay_mask(tile)

    st = {}
    for li in range(depth):
        j = li // 2
        xs, wg, wu, wo = _ffn_cast(xs, norm_g[li, 0], ffn1_w_in, ffn1_w_out, li)
        if li % 2 == 1:
            xp = _ffn(xp, norm_g[li, 0], wg, wu, wo)
        if li % 2 == 0:
            zs, w_in = _norm_proj_cast(xs, norm_g[li, 1], even_w_in, j, EVEN_IN_CHUNK)
            shared = (a_conv_w[j], a_conv_b[j].reshape(1, w_a),
                      _block_diag_gates(a_w_r[j], a_w_i[j], 4),
                      jnp.stack([a_b_r[j], a_b_i[j]]), a_lambda[j].reshape(1, w_a),
                      b_qk_norm[j, 0].reshape(1, HD_B), jnp.tile(b_qk_norm[j, 1], KV_B).reshape(1, nkv))
            ys, conv_s, h_s, k_s, v_s = _even_sample(
                zs, state_a_conv[j], state_a_h[j], cache_b_k[j].reshape(db, WINDOW, nkv),
                cache_b_v[j].reshape(db, WINDOW, nkv), *shared, bias_s, _sink_rows(b_sinks[j], dl), dl=dl)
            xs, w_out = _out_proj_cast(ys, even_w_out, j, xs)
            xp, conv_p, h_p, k_p, v_p = _even_prompt(
                xp, norm_g[li, 0].reshape(1, d), wg, wu, wo,
                norm_g[li, 1].reshape(1, d), w_in, w_out, *shared[:5],
                jnp.tile(b_qk_norm[j, 0], H_B).reshape(1, H_B * HD_B), shared[6],
                bias_p, _sink_rows(b_sinks[j], WINDOW, LANES))
            st.setdefault('a_conv', ([], []))
            st['a_conv'][0].append(conv_p[SUBLANES - (CONV_W - 1):].reshape(1, CONV_W - 1, w_a))
            st['a_conv'][1].append(conv_s)
            st.setdefault('a_h', ([], []))
            st['a_h'][0].append(h_p[0:1])
            st['a_h'][1].append(h_s.reshape(db, w_a))
            st.setdefault('b_k', ([], []))
            st['b_k'][0].append(k_p.reshape(1, WINDOW, KV_B, HD_B))
            st['b_k'][1].append(k_s.reshape(db, WINDOW, KV_B, HD_B))
            st.setdefault('b_v', ([], []))
            st['b_v'][0].append(v_p.reshape(1, WINDOW, KV_B, HD_B))
            st['b_v'][1].append(v_s.reshape(db, WINDOW, KV_B, HD_B))
        else:
            zs, w_in = _norm_proj_cast(xs, norm_g[li, 1], odd_w_in, j, ODD_IN_CHUNK)
            gb = jnp.pad(d_gate_b[j], (0, LANES - 2 * H_D)).reshape(1, LANES)
            shared = (d_conv_w[j], d_conv_b[j].reshape(1, wqk), gb)
            m0 = jnp.pad(state_d_m[j], ((0, 0), (0, LANES - H_D))).reshape(db, 1, LANES)
            ys, s_s, dconv_s, c_s, n_s, m_s = _odd_sample(zs, cos_s, sin_s, state_c_S[j], state_d_conv[j],
                                                          state_d_C[j], state_d_n[j], m0, *shared, dl=dl)
            xs, w_out = _out_proj_cast(ys, odd_w_out, j, xs)
            xp, dconv_p, s_p, c_p, n_p, m_p = _odd_prompt(xp, norm_g[li, 1].reshape(1, d), w_in, w_out,
                                                          rbase_p, roff_p, dmask, *shared)
            st.setdefault('c_S', ([], []))
            st['c_S'][0].append(s_p[None])
            st['c_S'][1].append(s_s)
            st.setdefault('d_conv', ([], []))
            st['d_conv'][0].append(dconv_p[SUBLANES - (CONV_W - 1):].reshape(1, CONV_W - 1, wqk))
            st['d_conv'][1].append(dconv_s)
            st.setdefault('d_C', ([], []))
            st['d_C'][0].append(c_p[None])
            st['d_C'][1].append(c_s)
            st.setdefault('d_n', ([], []))
            st['d_n'][0].append(n_p[None, :H_D])
            st['d_n'][1].append(n_s)
            st.setdefault('d_m', ([], []))
            st['d_m'][0].append(m_p[:H_D, 0].reshape(1, H_D))
            st['d_m'][1].append(m_s[:, 0, :H_D])
        xs, wg, wu, wo = _ffn_cast(xs, norm_g[li, 2], ffn2_w_in, ffn2_w_out, li)
        xp = _ffn(xp, norm_g[li, 2], wg, wu, wo)

    outs = [xp.reshape(1, lp, d), xs.reshape(db, dl, d)]
    for name in ('a_conv', 'a_h', 'b_k', 'b_v', 'c_S', 'd_conv', 'd_C', 'd_n', 'd_m'):
        outs.append(jnp.stack(st[name][0]))
        outs.append(jnp.stack(st[name][1]))
    return tuple(outs)
```

```python
import functools
import math

import jax
import jax.numpy as jnp
import numpy as np
from jax import lax
from jax.experimental import pallas as pl
from jax.experimental.pallas import tpu as pltpu

F32 = jnp.float32
BF16 = jnp.bfloat16

PAST_LEN = 16384
EPS = 1e-6
CONV_W = 4
C_A = 8.0
NB_A = 16
H_B, KV_B, HD_B = 8, 2, 64
G_B = H_B // KV_B
WINDOW = 128
N_BUCKETS = 32
MAX_DIST = 128
H_C, DK_C, DV_C = 4, 128, 256
H_D, DK_D, DV_D = 4, 128, 256
ROPE_BASE = 10000.0
NEG = -1e30

LANES = 128
SUBLANES = 8
VMEM_LIMIT = 56 * 1024 * 1024
VMEM_LIMIT_MAX = 60000 * 1024

FFN_TILE = 1024
MIX_TILE = 256
FF_CHUNK = 256
EVEN_SB = 32
ODD_SB = 8
PROJ_PIECE = 256
EVEN_IN_CHUNK = 1408
ODD_IN_CHUNK = 896


def _params(n_grid_dims=1, vmem_limit=VMEM_LIMIT):
    return pltpu.CompilerParams(dimension_semantics=("arbitrary",) * n_grid_dims,
                                vmem_limit_bytes=vmem_limit)


def _resident(shape):
    nd = len(shape)
    return pl.BlockSpec(shape, lambda i, _nd=nd: (0,) * _nd, pipeline_mode=pl.Buffered(1))


def _dot(a, b):
    return jnp.dot(a.astype(BF16), b.astype(BF16), preferred_element_type=F32)


def _dot_nt(a, b):
    return lax.dot_general(a.astype(BF16), b.astype(BF16), (((1,), (1,)), ((), ())),
                           preferred_element_type=F32)


def _dot_tn(a, b):
    return lax.dot_general(a.astype(BF16), b.astype(BF16), (((0,), (0,)), ((), ())),
                           preferred_element_type=F32)


def _rms_scale(x):
    return lax.rsqrt(jnp.mean(x * x, axis=-1, keepdims=True) + EPS)


def _sigmoid(x):
    return 1.0 / (1.0 + jnp.exp(-x))


def _silu(x):
    return x * _sigmoid(x)


def _log_sigmoid(x):
    return jnp.minimum(x, 0.0) - jnp.log1p(jnp.exp(-jnp.abs(x)))


def _gelu_tanh(x):
    k = 2.0 * math.sqrt(2.0 / math.pi)
    return x / (1.0 + jnp.exp(x * (-k - (k * 0.044715) * (x * x))))


def _sqrt_nonneg(x):
    return jnp.where(x > 0.0, x * lax.rsqrt(x), 0.0)


def _ffn_body(x_ref, *refs, n_blocks, d_ff, chunk):
    o_ref = refs[4 * n_blocks]
    x = x_ref[...]
    for b in range(n_blocks):
        g_ref, wg_ref, wu_ref, wo_ref = refs[4 * b:4 * b + 4]
        nb = (x * _rms_scale(x) * g_ref[...]).astype(BF16)
        acc = None
        for c0 in range(0, d_ff, chunk):
            gate = jnp.dot(nb, wg_ref[:, c0:c0 + chunk], preferred_element_type=F32)
            up = jnp.dot(nb, wu_ref[:, c0:c0 + chunk], preferred_element_type=F32)
            mid = (_silu(gate) * up).astype(BF16)
            part = jnp.dot(mid, wo_ref[c0:c0 + chunk, :], preferred_element_type=F32)
            acc = part if acc is None else acc + part
        x = x + 0.5 * acc
    o_ref[...] = x


def _ffn(x, *blocks):
    n, d = x.shape
    d_ff = blocks[0][3].shape[0]
    tm = min(FFN_TILE // len(blocks), n)
    assert n % tm == 0 and d_ff % FF_CHUNK == 0
    flat = [a for g, wg, wu, wo in blocks for a in (g.reshape(1, d), wg, wu, wo)]
    return pl.pallas_call(
        functools.partial(_ffn_body, n_blocks=len(blocks), d_ff=d_ff, chunk=FF_CHUNK),
        grid=(n // tm,),
        in_specs=[pl.BlockSpec((tm, d), lambda i: (i, 0))] + [_resident(a.shape) for a in flat],
        out_specs=pl.BlockSpec((tm, d), lambda i: (i, 0)),
        out_shape=jax.ShapeDtypeStruct((n, d), F32),
        compiler_params=_params(), name="ffn")(x, *flat)


def _ffn_cast_body(x_ref, g_ref, wg_ref, wu_ref, wo_ref, o_ref, wgb_ref, wub_ref, wob_ref, nb_ref, acc_ref):
    j = pl.program_id(0)

    @pl.when(j == 0)
    def _():
        x = x_ref[...]
        nb_ref[...] = (x * _rms_scale(x) * g_ref[...]).astype(BF16)
        acc_ref[...] = jnp.zeros_like(acc_ref)

    wg = wg_ref[...].astype(BF16)
    wu = wu_ref[...].astype(BF16)
    wo = wo_ref[...].astype(BF16)
    wgb_ref[...] = wg
    wub_ref[...] = wu
    wob_ref[...] = wo
    nb = nb_ref[...]
    gate = jnp.dot(nb, wg, preferred_element_type=F32)
    up = jnp.dot(nb, wu, preferred_element_type=F32)
    acc_ref[...] += jnp.dot((_silu(gate) * up).astype(BF16), wo, preferred_element_type=F32)

    @pl.when(j == pl.num_programs(0) - 1)
    def _():
        o_ref[...] = x_ref[...] + 0.5 * acc_ref[...]


def _ffn_cast(x, g, w_in, w_out, li):
    n, d = x.shape
    d_ff = w_out.shape[1]
    nc = d_ff // FF_CHUNK
    assert d_ff % FF_CHUNK == 0
    return pl.pallas_call(
        _ffn_cast_body, grid=(nc,),
        in_specs=[pl.BlockSpec((n, d), lambda j: (0, 0)), pl.BlockSpec((1, d), lambda j: (0, 0)),
                  pl.BlockSpec((None, d, FF_CHUNK), lambda j: (li, 0, j)),
                  pl.BlockSpec((None, d, FF_CHUNK), lambda j: (li, 0, j + nc)),
                  pl.BlockSpec((None, FF_CHUNK, d), lambda j: (li, j, 0))],
        out_specs=[pl.BlockSpec((n, d), lambda j: (0, 0)),
                   pl.BlockSpec((d, FF_CHUNK), lambda j: (0, j)),
                   pl.BlockSpec((d, FF_CHUNK), lambda j: (0, j)),
                   pl.BlockSpec((FF_CHUNK, d), lambda j: (j, 0))],
        out_shape=[jax.ShapeDtypeStruct((n, d), F32), jax.ShapeDtypeStruct((d, d_ff), BF16),
                   jax.ShapeDtypeStruct((d, d_ff), BF16), jax.ShapeDtypeStruct((d_ff, d), BF16)],
        scratch_shapes=[pltpu.VMEM((n, d), BF16), pltpu.VMEM((n, d), F32)],
        compiler_params=_params(), name="ffn_cast")(x, g.reshape(1, d), w_in, w_in, w_out)


def _ffn_cast2_body(x_ref, ga_ref, wga_ref, wua_ref, woa_ref, gb_ref, wgb_ref, wub_ref, wob_ref,
                    o_ref, oga_ref, oua_ref, ooa_ref, ogb_ref, oub_ref, oob_ref, xc_ref, nb_ref, acc_ref, *, nc):
    j = pl.program_id(0)

    def start(g_ref):
        x = xc_ref[...]
        nb_ref[...] = (x * _rms_scale(x) * g_ref[...]).astype(BF16)
        acc_ref[...] = jnp.zeros_like(acc_ref)

    def chunk(wg_ref, wu_ref, wo_ref, og_ref, ou_ref, oo_ref):
        wg = wg_ref[...].astype(BF16)
        wu = wu_ref[...].astype(BF16)
        wo = wo_ref[...].astype(BF16)
        og_ref[...] = wg
        ou_ref[...] = wu
        oo_ref[...] = wo
        nb = nb_ref[...]
        gate = jnp.dot(nb, wg, preferred_element_type=F32)
        up = jnp.dot(nb, wu, preferred_element_type=F32)
        acc_ref[...] += jnp.dot((_silu(gate) * up).astype(BF16), wo, preferred_element_type=F32)

    @pl.when(j == 0)
    def _():
        xc_ref[...] = x_ref[...]
        start(ga_ref)

    @pl.when(j < nc)
    def _():
        chunk(wga_ref, wua_ref, woa_ref, oga_ref, oua_ref, ooa_ref)

    @pl.when(j == nc)
    def _():
        xc_ref[...] = xc_ref[...] + 0.5 * acc_ref[...]
        start(gb_ref)

    @pl.when(j >= nc)
    def _():
        chunk(wgb_ref, wub_ref, wob_ref, ogb_ref, oub_ref, oob_ref)

    @pl.when(j == 2 * nc - 1)
    def _():
        o_ref[...] = xc_ref[...] + 0.5 * acc_ref[...]


def _ffn_cast2(x, ga, wa_in, wa_out, la, gb, wb_in, wb_out, lb):
    n, d = x.shape
    d_ff = wa_out.shape[1]
    nc = d_ff // FF_CHUNK
    assert d_ff % FF_CHUNK == 0 and wb_out.shape[1] == d_ff
    ca = lambda j: jnp.minimum(j, nc - 1)
    cb = lambda j: jnp.maximum(j - nc, 0)

    def w_specs(layer, c):
        return [pl.BlockSpec((1, d), lambda j: (0, 0)),
                pl.BlockSpec((None, d, FF_CHUNK), lambda j: (layer, 0, c(j))),
                pl.BlockSpec((None, d, FF_CHUNK), lambda j: (layer, 0, c(j) + nc)),
                pl.BlockSpec((None, FF_CHUNK, d), lambda j: (layer, c(j), 0))]

    def o_specs(c):
        return [pl.BlockSpec((d, FF_CHUNK), lambda j: (0, c(j))), pl.BlockSpec((d, FF_CHUNK), lambda j: (0, c(j))),
                pl.BlockSpec((FF_CHUNK, d), lambda j: (c(j), 0))]

    w_shapes = [jax.ShapeDtypeStruct((d, d_ff), BF16), jax.ShapeDtypeStruct((d, d_ff), BF16),
                jax.ShapeDtypeStruct((d_ff, d), BF16)]
    outs = pl.pallas_call(
        functools.partial(_ffn_cast2_body, nc=nc), grid=(2 * nc,),
        in_specs=[pl.BlockSpec((n, d), lambda j: (0, 0))] + w_specs(la, ca) + w_specs(lb, cb),
        out_specs=[pl.BlockSpec((n, d), lambda j: (0, 0))] + o_specs(ca) + o_specs(cb),
        out_shape=[jax.ShapeDtypeStruct((n, d), F32)] + w_shapes + w_shapes,
        scratch_shapes=[pltpu.VMEM((n, d), F32), pltpu.VMEM((n, d), BF16), pltpu.VMEM((n, d), F32)],
        compiler_params=_params(), name="ffn_cast2")(
            x, ga.reshape(1, d), wa_in, wa_in, wa_out, gb.reshape(1, d), wb_in, wb_in, wb_out)
    return outs[0], tuple(outs[1:4]), tuple(outs[4:7])


def _norm_proj_cast_body(x_ref, g_ref, w_ref, z_ref, wb_ref, nb_ref, *, chunk, valid):
    j = pl.program_id(0)

    @pl.when(j == 0)
    def _():
        x = x_ref[...]
        nb_ref[...] = (x * _rms_scale(x) * g_ref[...]).astype(BF16)

    col = j * chunk + lax.broadcasted_iota(jnp.int32, w_ref.shape, 1)
    w = jnp.where(col < valid, w_ref[...], 0.0).astype(BF16)
    wb_ref[...] = w
    z_ref[...] = jnp.dot(nb_ref[...], w, preferred_element_type=F32)


def _norm_proj_cast(x, g, w_all, li, chunk):
    n, d = x.shape
    m = w_all.shape[2]
    nc = -(-m // chunk)
    mp = nc * chunk
    return pl.pallas_call(
        functools.partial(_norm_proj_cast_body, chunk=chunk, valid=m), grid=(nc,),
        in_specs=[pl.BlockSpec((n, d), lambda j: (0, 0)), pl.BlockSpec((1, d), lambda j: (0, 0)),
                  pl.BlockSpec((None, d, chunk), lambda j: (li, 0, j))],
        out_specs=[pl.BlockSpec((n, chunk), lambda j: (0, j)), pl.BlockSpec((d, chunk), lambda j: (0, j))],
        out_shape=[jax.ShapeDtypeStruct((n, mp), F32), jax.ShapeDtypeStruct((d, mp), BF16)],
        scratch_shapes=[pltpu.VMEM((n, d), BF16)],
        compiler_params=_params(), name="norm_proj_cast")(x, g.reshape(1, d), w_all)


def _out_proj_cast_body(y_ref, w_ref, x_ref, o_ref, wb_ref):
    w = w_ref[...].astype(BF16)
    wb_ref[...] = w
    o_ref[...] = x_ref[...] + jnp.dot(y_ref[...], w, preferred_element_type=F32)


def _out_proj_cast(y, w_all, li, x):
    n, k = y.shape
    d = w_all.shape[2]
    chunk = PROJ_PIECE
    assert d % chunk == 0
    return pl.pallas_call(
        _out_proj_cast_body, grid=(d // chunk,),
        in_specs=[pl.BlockSpec((n, k), lambda j: (0, 0)),
                  pl.BlockSpec((None, k, chunk), lambda j: (li, 0, j)),
                  pl.BlockSpec((n, chunk), lambda j: (0, j))],
        out_specs=[pl.BlockSpec((n, chunk), lambda j: (0, j)), pl.BlockSpec((k, chunk), lambda j: (0, j))],
        out_shape=[jax.ShapeDtypeStruct((n, d), F32), jax.ShapeDtypeStruct((k, d), BF16)],
        compiler_params=_params(), name="out_proj_cast")(y, w_all, x)


class _SideWork:
    def __init__(self, pieces=(), stages=1):
        self.pieces, self.stages, self.done, self.stage = list(pieces), stages, 0, 0

    def tick(self):
        self.stage += 1
        upto = min(len(self.pieces), -(-len(self.pieces) * self.stage // self.stages))
        while self.done < upto:
            self.pieces[self.done]()
            self.done += 1

    def flush(self):
        self.stage = self.stages - 1
        self.tick()


def _norm_piece(x_ref, r0, r1, g_ref, nb_ref):
    x = x_ref[r0:r1, :]
    nb_ref[...] = (x * _rms_scale(x) * g_ref[...]).astype(BF16)


def _proj_piece(nb_ref, w_ref, z_ref, c0, c1):
    z_ref[:, c0:c1] = jnp.dot(nb_ref[...], w_ref[:, c0:c1], preferred_element_type=F32)


def _proj_pieces(x_ref, r0, r1, g_ref, nb_ref, w_ref, z_ref, width):
    m = w_ref.shape[1]
    return [functools.partial(_norm_piece, x_ref, r0, r1, g_ref, nb_ref)] + [
        functools.partial(_proj_piece, nb_ref, w_ref, z_ref, c0, min(c0 + width, m))
        for c0 in range(0, m, width)]


def _out_piece(y_ref, y_val, w_ref, x_ref, xr0, o_ref, or0, rows, c0, c1):
    if not y_val:
        y_val.append(y_ref[...])
    o_ref[or0:or0 + rows, c0:c1] = x_ref[xr0:xr0 + rows, c0:c1] + jnp.dot(
        y_val[0], w_ref[:, c0:c1], preferred_element_type=F32)


def _out_pieces(y_ref, w_ref, x_ref, xr0, o_ref, or0, rows, width):
    m = w_ref.shape[1]
    y_val = []
    return [functools.partial(_out_piece, y_ref, y_val, w_ref, x_ref, xr0, o_ref, or0, rows,
                              c0, min(c0 + width, m)) for c0 in range(0, m, width)]


def _ffn_up_piece(nb_ref, wg_ref, wu_ref, mid_ref, c0, c1):
    nb = nb_ref[...]
    gate = jnp.dot(nb, wg_ref[:, c0:c1], preferred_element_type=F32)
    up = jnp.dot(nb, wu_ref[:, c0:c1], preferred_element_type=F32)
    mid_ref[...] = (_silu(gate) * up).astype(BF16)


def _ffn_down_piece(mid_ref, wo_ref, acc_ref, c0, c1):
    part = jnp.dot(mid_ref[...], wo_ref[c0:c1, :], preferred_element_type=F32)
    if c0 == 0:
        acc_ref[...] = part
    else:
        acc_ref[...] += part


def _ffn_final_piece(x_ref, r0, r1, acc_ref, o_ref):
    o_ref[...] = x_ref[r0:r1, :] + 0.5 * acc_ref[...]


def _ffn_pieces(x_ref, r0, r1, g_ref, nb_ref, wg_ref, wu_ref, wo_ref, mid_refs, acc_ref, o_ref, width):
    d_ff = wo_ref.shape[0]
    assert d_ff % width == 0
    ups = [functools.partial(_ffn_up_piece, nb_ref, wg_ref, wu_ref, mid_refs[(c0 // width) % 2], c0, c0 + width)
           for c0 in range(0, d_ff, width)]
    downs = [functools.partial(_ffn_down_piece, mid_refs[(c0 // width) % 2], wo_ref, acc_ref, c0, c0 + width)
             for c0 in range(0, d_ff, width)]
    order = [ups[0]]
    for c in range(1, len(ups)):
        order += [ups[c], downs[c - 1]]
    order.append(downs[-1])
    return ([functools.partial(_norm_piece, x_ref, r0, r1, g_ref, nb_ref)] + order
            + [functools.partial(_ffn_final_piece, x_ref, r0, r1, acc_ref, o_ref)])


def _rglru_gate_groups(y, wri_ref, bri_ref, lam_ref):
    w = y.shape[1]
    gw = wri_ref.shape[1]
    yb = y.astype(BF16)
    logsig = _log_sigmoid(lam_ref[...])
    for g in range(w // gw):
        cols = slice(g * gw, (g + 1) * gw)
        ri = jnp.dot(yb[:, cols], wri_ref[g], preferred_element_type=F32)
        r = _sigmoid(ri[:, :gw] + bri_ref[0:1, cols])
        i = _sigmoid(ri[:, gw:] + bri_ref[1:2, cols])
        log_a = C_A * r * logsig[:, cols]
        th = jnp.tanh(log_a)
        one_minus_a2 = -2.0 * th / (1.0 - th)
        yield jnp.exp(log_a), _sqrt_nonneg(one_minus_a2) * (i * y[:, cols])


def _rglru_gates(y, wri_ref, bri_ref, lam_ref):
    parts = list(_rglru_gate_groups(y, wri_ref, bri_ref, lam_ref))
    return jnp.concatenate([p[0] for p in parts], axis=1), jnp.concatenate([p[1] for p in parts], axis=1)


def _group_scan(a3, u3):
    t = lax.broadcasted_iota(jnp.int32, a3.shape, 1)
    s = 1
    while s < SUBLANES:
        keep = t >= s
        u3 = jnp.where(keep, a3 * pltpu.roll(u3, s, axis=1) + u3, u3)
        a3 = jnp.where(keep, a3 * pltpu.roll(a3, s, axis=1), a3)
        s *= 2
    return a3, u3


EVEN_TILE_STAGES = 16


def _even_rglru_branch(z_ref, y_ref, cw_ref, cb_ref, wri_ref, bri_ref, lam_ref, conv_ref, h_ref,
                       xbuf, hcar, hbuf, *, tile, w_a):
    xa = z_ref[:, 0:w_a]
    xbuf[SUBLANES:SUBLANES + tile, :] = xa
    cw = cw_ref[...]
    y = cb_ref[...] + cw[3:4] * xa
    for i in range(CONV_W - 1):
        off = SUBLANES - (CONV_W - 1) + i
        y = y + cw[i:i + 1] * xbuf[off:off + tile, :]
    xbuf[0:SUBLANES, :] = xbuf[tile:tile + SUBLANES, :]
    conv_ref[...] = xbuf[0:SUBLANES, :]
    yield

    a_parts, u_parts = [], []
    for a_g, u_g in _rglru_gate_groups(y, wri_ref, bri_ref, lam_ref):
        a_parts.append(a_g)
        u_parts.append(u_g)
        yield
    a = jnp.concatenate(a_parts, axis=1)
    u = jnp.concatenate(u_parts, axis=1)

    ng = tile // SUBLANES
    a3, u3 = _group_scan(a.reshape(ng, SUBLANES, w_a), u.reshape(ng, SUBLANES, w_a))
    yield
    carry = hcar[0:1, :]
    for g in range(ng):
        hg = u3[g] + a3[g] * carry
        hbuf[g * SUBLANES:(g + 1) * SUBLANES, :] = hg
        carry = hg[SUBLANES - 1:SUBLANES, :]
    hcar[0:1, :] = carry
    h_ref[...] = jnp.broadcast_to(carry, h_ref.shape)
    yield
    y_ref[:, 0:w_a] = (hbuf[...] * _gelu_tanh(z_ref[:, w_a:2 * w_a])).astype(BF16)
    yield


def _head_mean_sq(x):
    x2 = x * x
    lo = lax.broadcasted_iota(jnp.int32, (x.shape[0], LANES), 1) < HD_B
    groups = []
    for v in range(x.shape[1] // LANES):
        blk = x2[:, v * LANES:(v + 1) * LANES]
        s_lo = jnp.sum(jnp.where(lo, blk, 0.0), axis=-1, keepdims=True)
        s_hi = jnp.sum(jnp.where(lo, 0.0, blk), axis=-1, keepdims=True)
        groups.append(jnp.where(lo, s_lo, s_hi))
    return jnp.concatenate(groups, axis=1) * (1.0 / HD_B)


def _even_attn_branch(z_ref, y_ref, is_first, qg_ref, kg_ref, bias_ref, sink_ref, kl_ref, vl_ref,
                      kbuf, vbuf, *, tile, w_a):
    nq = H_B * HD_B
    nkv = KV_B * HD_B
    k = z_ref[:, 2 * w_a + nq:2 * w_a + nq + nkv]
    kbuf[WINDOW:WINDOW + tile, :] = k * lax.rsqrt(_head_mean_sq(k) + EPS) * kg_ref[...]
    vbuf[WINDOW:WINDOW + tile, :] = z_ref[:, 2 * w_a + nq + nkv:2 * w_a + nq + 2 * nkv]
    yield

    chains = [(nb, kv) for nb in range(tile // WINDOW) for kv in range(KV_B)]
    rows = G_B * WINDOW
    q_all = z_ref[:, 2 * w_a:2 * w_a + nq]
    q_all = q_all * lax.rsqrt(_head_mean_sq(q_all) + EPS) * qg_ref[...] * (HD_B ** -0.5)
    qs = jnp.concatenate([q_all[nb * WINDOW:(nb + 1) * WINDOW, (kv * G_B + g) * HD_B:(kv * G_B + g + 1) * HD_B]
                          for nb, kv in chains for g in range(G_B)], axis=0).astype(BF16)
    yield
    col = lax.broadcasted_iota(jnp.int32, (rows, 2 * WINDOW), 1)
    s_parts = []
    for c, (nb, kv) in enumerate(chains):
        kk = kbuf[nb * WINDOW:(nb + 2) * WINDOW, kv * HD_B:(kv + 1) * HD_B]
        s = _dot_nt(qs[c * rows:(c + 1) * rows], kk) + bias_ref[kv]
        if nb == 0 and is_first is not False:
            s = jnp.where(jnp.logical_and(is_first, col < WINDOW), NEG, s)
        s_parts.append(s)
        yield
    s = jnp.concatenate(s_parts, axis=0)
    sink = jnp.concatenate([sink_ref[kv] for _, kv in chains], axis=0)
    m = jnp.maximum(jnp.max(s, axis=-1, keepdims=True), sink)
    p = jnp.exp(s - jnp.concatenate([m] * (2 * WINDOW // LANES), axis=1))
    inv = (1.0 / (jnp.sum(p, axis=-1, keepdims=True) + jnp.exp(sink - m)))[:, 0:HD_B]
    p = p.astype(BF16)
    yield
    o = jnp.concatenate([_dot(p[c * rows:(c + 1) * rows],
                              vbuf[nb * WINDOW:(nb + 2) * WINDOW, kv * HD_B:(kv + 1) * HD_B])
                         for c, (nb, kv) in enumerate(chains)], axis=0) * inv
    y_ref[:, w_a:w_a + nq] = jnp.concatenate(
        [jnp.concatenate([o[(c * G_B + g) * WINDOW:(c * G_B + g + 1) * WINDOW, :]
                          for c, (cb_, _) in enumerate(chains) if cb_ == nb for g in range(G_B)], axis=1)
         for nb in range(tile // WINDOW)], axis=0).astype(BF16)

    kbuf[0:WINDOW, :] = kbuf[tile:tile + WINDOW, :]
    vbuf[0:WINDOW, :] = vbuf[tile:tile + WINDOW, :]
    kl_ref[...] = kbuf[0:WINDOW, :]
    vl_ref[...] = vbuf[0:WINDOW, :]
    yield


def _even_tile(z_ref, y_ref, is_first, side, cw_ref, cb_ref, wri_ref, bri_ref, lam_ref, qg_ref, kg_ref,
               bias_ref, sink_ref, conv_ref, h_ref, kl_ref, vl_ref, xbuf, hcar, hbuf, kbuf, vbuf, *, tile, w_a):
    branches = [
        _even_rglru_branch(z_ref, y_ref, cw_ref, cb_ref, wri_ref, bri_ref, lam_ref, conv_ref, h_ref,
                           xbuf, hcar, hbuf, tile=tile, w_a=w_a),
        _even_attn_branch(z_ref, y_ref, is_first, qg_ref, kg_ref, bias_ref, sink_ref, kl_ref, vl_ref,
                          kbuf, vbuf, tile=tile, w_a=w_a)]
    while branches:
        for b in list(branches):
            if next(b, StopIteration) is StopIteration:
                branches.remove(b)
            else:
                side.tick()


def _even_prompt_body(x_ref, xn_ref, gf_ref, wg_ref, wu_ref, wo_ref, g_ref, win_ref, wout_ref, cw_ref, cb_ref,
                      wri_ref, bri_ref, lam_ref, qg_ref, kg_ref, bias_ref, sink_ref,
                      o_ref, conv_ref, h_ref, kl_ref, vl_ref,
                      zb0, zb1, yb0, yb1, x1b0, x1b1, acc, mid0, mid1, nbuf, xbuf, hcar, hbuf, kbuf, vbuf, *,
                      tile, w_a):
    step = pl.program_id(0)
    nkv = KV_B * HD_B

    def ffn_proj(src_ref, r0, x1b, zb):
        return (_ffn_pieces(src_ref, r0, r0 + tile, gf_ref, nbuf, wg_ref, wu_ref, wo_ref, (mid0, mid1), acc, x1b,
                            FF_CHUNK)
                + _proj_pieces(x1b, 0, tile, g_ref, nbuf, win_ref, zb, PROJ_PIECE))

    @pl.when(step == 0)
    def _():
        xbuf[0:SUBLANES, :] = jnp.zeros((SUBLANES, w_a), F32)
        hcar[...] = jnp.zeros_like(hcar)
        kbuf[0:WINDOW, :] = jnp.zeros((WINDOW, nkv), F32)
        vbuf[0:WINDOW, :] = jnp.zeros((WINDOW, nkv), F32)
        _SideWork(ffn_proj(x_ref, 0, x1b0, zb0)).flush()

    mix = functools.partial(_even_tile, cw_ref=cw_ref, cb_ref=cb_ref, wri_ref=wri_ref, bri_ref=bri_ref,
                            lam_ref=lam_ref, qg_ref=qg_ref, kg_ref=kg_ref, bias_ref=bias_ref,
                            sink_ref=sink_ref, conv_ref=conv_ref, h_ref=h_ref, kl_ref=kl_ref, vl_ref=vl_ref,
                            xbuf=xbuf, hcar=hcar, hbuf=hbuf, kbuf=kbuf, vbuf=vbuf, tile=tile, w_a=w_a)
    side = _SideWork(ffn_proj(x_ref, tile, x1b1, zb1), EVEN_TILE_STAGES)
    mix(zb0, yb0, step == 0, side)
    side.flush()
    side = _SideWork(_out_pieces(yb0, wout_ref, x1b0, 0, o_ref, 0, tile, PROJ_PIECE)
                     + ffn_proj(xn_ref, 0, x1b0, zb0), EVEN_TILE_STAGES)
    mix(zb1, yb1, False, side)
    side.flush()
    _SideWork(_out_pieces(yb1, wout_ref, x1b1, 0, o_ref, tile, tile, PROJ_PIECE)).flush()


def _even_prompt(x, gf, wg, wu, wo, g, w_in, w_out, cw, cb, wri, bri, lam, qg, kg, bias, sink):
    n, d = x.shape
    zin = w_in.shape[1]
    w_a = cw.shape[1]
    tile = min(MIX_TILE, n // 2)
    assert n % (2 * tile) == 0 and tile % WINDOW == 0
    nt = n // tile
    nq, nkv = H_B * HD_B, KV_B * HD_B
    small = (gf, wg, wu, wo, g, w_in, w_out, cw, cb, wri, bri, lam, qg, kg, bias, sink)
    return pl.pallas_call(
        functools.partial(_even_prompt_body, tile=tile, w_a=w_a),
        grid=(nt // 2,),
        in_specs=[pl.BlockSpec((2 * tile, d), lambda i: (i, 0)),
                  pl.BlockSpec((tile, d), lambda i: (jnp.minimum(2 * i + 2, nt - 1), 0))]
                 + [_resident(a.shape) for a in small],
        out_specs=[pl.BlockSpec((2 * tile, d), lambda i: (i, 0)),
                   pl.BlockSpec((SUBLANES, w_a), lambda i: (0, 0)),
                   pl.BlockSpec((SUBLANES, w_a), lambda i: (0, 0)),
                   pl.BlockSpec((WINDOW, nkv), lambda i: (0, 0)),
                   pl.BlockSpec((WINDOW, nkv), lambda i: (0, 0))],
        out_shape=[jax.ShapeDtypeStruct((n, d), F32),
                   jax.ShapeDtypeStruct((SUBLANES, w_a), F32),
                   jax.ShapeDtypeStruct((SUBLANES, w_a), F32),
                   jax.ShapeDtypeStruct((WINDOW, nkv), F32),
                   jax.ShapeDtypeStruct((WINDOW, nkv), F32)],
        scratch_shapes=[pltpu.VMEM((tile, zin), F32), pltpu.VMEM((tile, zin), F32),
                        pltpu.VMEM((tile, w_a + nq), BF16), pltpu.VMEM((tile, w_a + nq), BF16),
                        pltpu.VMEM((tile, d), F32), pltpu.VMEM((tile, d), F32), pltpu.VMEM((tile, d), F32),
                        pltpu.VMEM((tile, FF_CHUNK), BF16), pltpu.VMEM((tile, FF_CHUNK), BF16),
                        pltpu.VMEM((tile, d), BF16),
                        pltpu.VMEM((tile + SUBLANES, w_a), F32), pltpu.VMEM((SUBLANES, w_a), F32),
                        pltpu.VMEM((tile, w_a), F32),
                        pltpu.VMEM((tile + WINDOW, nkv), F32), pltpu.VMEM((tile + WINDOW, nkv), F32)],
        compiler_params=_params(vmem_limit=VMEM_LIMIT_MAX), name="even_prompt")(x, x, *small)


def _even_sample_body(z_ref, conv_ref, h0_ref, ck_ref, cv_ref, cw_ref, cb_ref, wri_ref, bri_ref, lam_ref,
                      qg_ref, kg_ref, bias_ref, sink_ref, y_ref, co_ref, hl_ref, ko_ref, vo_ref, xc, *, sb, dl, w_a):
    nq = H_B * HD_B
    nkv = KV_B * HD_B
    rows = sb * dl
    nkeys = WINDOW + dl

    xa3 = z_ref[:, 0:w_a].reshape(sb, dl, w_a)
    xc[:, SUBLANES:SUBLANES + dl, :] = xa3
    xc[:, SUBLANES - (CONV_W - 1):SUBLANES, :] = conv_ref[...]
    cw = cw_ref[...]
    y3 = cb_ref[...] + cw[3:4] * xa3
    for i in range(CONV_W - 1):
        off = SUBLANES - (CONV_W - 1) + i
        y3 = y3 + cw[i:i + 1] * xc[:, off:off + dl, :]
    y = y3.reshape(rows, w_a)

    a, u = _rglru_gates(y, wri_ref, bri_ref, lam_ref)
    a3 = a.reshape(sb, dl, w_a)
    u3 = u.reshape(sb, dl, w_a)
    t = lax.broadcasted_iota(jnp.int32, a3.shape, 1)
    h0 = jnp.broadcast_to(h0_ref[...][:, None, :], a3.shape)
    u3 = jnp.where(t == 0, u3 + a3 * h0, u3)
    _, h3 = _group_scan(a3, u3)
    hs = h3.reshape(rows, w_a)
    hl_ref[...] = h3[:, dl - 1:dl, :]
    co_ref[...] = xa3[:, dl - (CONV_W - 1):dl, :]
    y_ref[:, 0:w_a] = (hs * _gelu_tanh(z_ref[:, w_a:2 * w_a])).astype(BF16)

    q3 = z_ref[:, 2 * w_a:2 * w_a + nq].reshape(sb, dl, nq)
    k3 = z_ref[:, 2 * w_a + nq:2 * w_a + nq + nkv].reshape(sb, dl, nkv)
    v3 = z_ref[:, 2 * w_a + nq + nkv:2 * w_a + nq + 2 * nkv].reshape(sb, dl, nkv)
    kparts = []
    for hh in range(KV_B):
        kh = k3[:, :, hh * HD_B:(hh + 1) * HD_B]
        kparts.append(kh * _rms_scale(kh))
    kn3 = jnp.concatenate(kparts, axis=2) * kg_ref[...]
    ko_ref[:, 0:WINDOW - dl, :] = ck_ref[:, dl:WINDOW, :]
    ko_ref[:, WINDOW - dl:WINDOW, :] = kn3
    vo_ref[:, 0:WINDOW - dl, :] = cv_ref[:, dl:WINDOW, :]
    vo_ref[:, WINDOW - dl:WINDOW, :] = v3

    outs = []
    for kv in range(KV_B):
        hs_ = slice(kv * HD_B, (kv + 1) * HD_B)
        qs = jnp.concatenate([q3[:, :, (kv * G_B + g) * HD_B:(kv * G_B + g + 1) * HD_B]
                              for g in range(G_B)], axis=1)
        qs = qs * _rms_scale(qs) * qg_ref[...] * (HD_B ** -0.5)
        kc = jnp.concatenate([ck_ref[:, :, hs_], kn3[:, :, hs_]], axis=1)
        vc = jnp.concatenate([cv_ref[:, :, hs_], v3[:, :, hs_]], axis=1)
        s = jnp.einsum('bqd,bkd->bqk', qs.astype(BF16), kc.astype(BF16),
                       preferred_element_type=F32) + bias_ref[kv]
        sink = sink_ref[kv]
        m = jnp.maximum(jnp.max(s, axis=-1, keepdims=True), sink)
        p = jnp.exp(s - m)
        den = jnp.sum(p, axis=-1, keepdims=True) + jnp.exp(sink - m)
        o = jnp.einsum('bqk,bkd->bqd', p.astype(BF16), vc.astype(BF16),
                       preferred_element_type=F32) / den
        outs += [o[:, g * dl:(g + 1) * dl, :] for g in range(G_B)]
    y_ref[:, w_a:w_a + nq] = jnp.concatenate(outs, axis=2).reshape(rows, nq).astype(BF16)


def _even_sample(z, conv, h0, ck, cv, cw, cb, wri, bri, lam, qg, kg, bias, sink, *, dl):
    n, zin = z.shape
    db = n // dl
    w_a = cw.shape[1]
    sb = min(EVEN_SB, db)
    assert db % sb == 0 and dl == SUBLANES
    nq, nkv = H_B * HD_B, KV_B * HD_B
    rows = sb * dl
    return pl.pallas_call(
        functools.partial(_even_sample_body, sb=sb, dl=dl, w_a=w_a),
        grid=(db // sb,),
        in_specs=[pl.BlockSpec((rows, zin), lambda i: (i, 0)),
                  pl.BlockSpec((sb, CONV_W - 1, w_a), lambda i: (i, 0, 0)),
                  pl.BlockSpec((sb, w_a), lambda i: (i, 0)),
                  pl.BlockSpec((sb, WINDOW, nkv), lambda i: (i, 0, 0)),
                  pl.BlockSpec((sb, WINDOW, nkv), lambda i: (i, 0, 0))]
                 + [_resident(a.shape) for a in (cw, cb, wri, bri, lam, qg, kg, bias, sink)],
        out_specs=[pl.BlockSpec((rows, w_a + nq), lambda i: (i, 0)),
                   pl.BlockSpec((sb, CONV_W - 1, w_a), lambda i: (i, 0, 0)),
                   pl.BlockSpec((sb, 1, w_a), lambda i: (i, 0, 0)),
                   pl.BlockSpec((sb, WINDOW, nkv), lambda i: (i, 0, 0)),
                   pl.BlockSpec((sb, WINDOW, nkv), lambda i: (i, 0, 0))],
        out_shape=[jax.ShapeDtypeStruct((n, w_a + nq), BF16),
                   jax.ShapeDtypeStruct((db, CONV_W - 1, w_a), F32),
                   jax.ShapeDtypeStruct((db, 1, w_a), F32),
                   jax.ShapeDtypeStruct((db, WINDOW, nkv), F32),
                   jax.ShapeDtypeStruct((db, WINDOW, nkv), F32)],
        scratch_shapes=[pltpu.VMEM((sb, 2 * SUBLANES, w_a), F32)],
        compiler_params=_params(), name="even_sample")(
            z, conv, h0, ck, cv, cw, cb, wri, bri, lam, qg, kg, bias, sink)


def _retention_log_decay():
    return [float(np.log1p(-np.exp2(np.float32(-5.0 - h)))) for h in range(H_C)]


def _rotate(x, cosf, sinf, axis):
    return x * cosf + pltpu.roll(x, x.shape[axis] // 2, axis=axis) * sinf


ODD_TILE_STAGES = 2 * H_C + 2 * H_D + 2


def _odd_tile(z_ref, y_ref, cosf, sinf, side, dmask_ref, cw_ref, cb_ref, gb_ref, s_s, c_s, n_s, m_s, qkbuf, *,
              tile):
    wq = H_C * DK_C
    wv = H_C * DV_C
    o_kc, o_vc, o_gc = wq, 2 * wq, 2 * wq + wv
    o_qk = 2 * wq + 2 * wv
    wqk = 2 * H_D * DK_D
    o_vd = o_qk + wqk
    o_od = o_vd + H_D * DV_D
    o_gt = o_od + H_D * DV_D
    lg = _retention_log_decay()
    ti = lax.broadcasted_iota(jnp.int32, (tile, 1), 0).astype(F32)

    for h in range(H_C):
        q = _rotate(z_ref[:, h * DK_C:(h + 1) * DK_C], cosf, sinf, 1)
        k = _rotate(z_ref[:, o_kc + h * DK_C:o_kc + (h + 1) * DK_C], cosf, sinf, 1) * (DK_C ** -0.5)
        v = z_ref[:, o_vc + h * DV_C:o_vc + (h + 1) * DV_C]
        xi = jnp.exp((ti + 1.0) * lg[h])
        zeta = jnp.exp((tile - 1.0 - ti) * lg[h])
        sc = _dot_nt(q, k) * dmask_ref[h]
        o = _dot(sc, v) + _dot(q * xi, s_s[h])
        side.tick()
        s_s[h] = math.exp(tile * lg[h]) * s_s[h] + _dot_tn(k * zeta, v)
        gate = z_ref[:, o_gc + h * DV_C:o_gc + (h + 1) * DV_C]
        y_ref[:, h * DV_C:(h + 1) * DV_C] = (o * _rms_scale(o) * _silu(gate)).astype(BF16)
        side.tick()

    xqk = z_ref[:, o_qk:o_qk + wqk]
    qkbuf[SUBLANES:SUBLANES + tile, :] = xqk
    cw = cw_ref[...]
    qk = cb_ref[...] + cw[3:4] * xqk
    for i in range(CONV_W - 1):
        off = SUBLANES - (CONV_W - 1) + i
        qk = qk + cw[i:i + 1] * qkbuf[off:off + tile, :]
    qkbuf[0:SUBLANES, :] = qkbuf[tile:tile + SUBLANES, :]
    qk = _silu(qk)
    side.tick()

    gates = z_ref[:, o_gt:o_gt + LANES] + gb_ref[...]
    logf = _log_sigmoid(gates)
    row = lax.broadcasted_iota(jnp.int32, gates.shape, 0)
    bsum = logf
    s = 1
    while s < tile:
        bsum = bsum + jnp.where(row >= s, pltpu.roll(bsum, s, axis=0), 0.0)
        s *= 2
    bsum_t = bsum.T
    gates_t = gates.T
    ii = lax.broadcasted_iota(jnp.int32, (tile, tile), 0)
    jj = lax.broadcasted_iota(jnp.int32, (tile, tile), 1)
    causal = jj <= ii
    side.tick()
    for h in range(H_D):
        q = qk[:, h * DK_D:(h + 1) * DK_D]
        k = qk[:, H_D * DK_D + h * DK_D:H_D * DK_D + (h + 1) * DK_D] * (DK_D ** -0.5)
        v = z_ref[:, o_vd + h * DV_D:o_vd + (h + 1) * DV_D]
        b_col = bsum[:, H_D + h:H_D + h + 1]
        b_row = bsum_t[H_D + h:H_D + h + 1, :]
        i_col = gates[:, h:h + 1]
        i_row = gates_t[h:h + 1, :]
        m_prev = m_s[h:h + 1, 0:1]
        dlog = jnp.where(causal, b_col - b_row + i_row, NEG)
        init_log = b_col + m_prev
        m_t = jnp.maximum(init_log, jnp.max(dlog, axis=-1, keepdims=True))
        w = jnp.exp(dlog - m_t)
        a0 = jnp.exp(init_log - m_t)
        sc = _dot_nt(q, k) * w
        num = _dot(sc, v) + a0 * _dot(q, c_s[h])
        den = jnp.sum(sc, axis=-1, keepdims=True) + a0 * jnp.sum(q * n_s[h:h + 1, :], axis=-1, keepdims=True)
        den = jnp.maximum(jnp.abs(den), jnp.exp(-m_t))
        hd = num / den
        og = z_ref[:, o_od + h * DV_D:o_od + (h + 1) * DV_D]
        y_ref[:, wv + h * DV_D:wv + (h + 1) * DV_D] = (hd * _sigmoid(og)).astype(BF16)
        side.tick()
        b_end = b_col[tile - 1:tile, :]
        log_end_col = b_end - b_col + i_col
        m_new = jnp.maximum(b_end + m_prev, jnp.max(log_end_col, axis=0, keepdims=True))
        w_end = jnp.exp(log_end_col - m_new)
        a_end = jnp.exp(b_end + m_prev - m_new)
        kw = k * w_end
        c_s[h] = a_end * c_s[h] + _dot_tn(kw, v)
        n_s[h:h + 1, :] = a_end * n_s[h:h + 1, :] + jnp.sum(kw, axis=0, keepdims=True)
        m_s[h:h + 1, :] = jnp.broadcast_to(m_new, (1, LANES))
        side.tick()


def _odd_prompt_body(x_ref, xn_ref, g_ref, win_ref, wout_ref, rbase_ref, roff_ref, dmask_ref, cw_ref, cb_ref,
                     gb_ref, o_ref, conv_ref, so_ref, co_ref, no_ref, mo_ref,
                     zb0, zb1, yb0, yb1, nbuf, s_s, c_s, n_s, m_s, qkbuf, *, tile):
    step = pl.program_id(0)

    @pl.when(step == 0)
    def _():
        s_s[...] = jnp.zeros_like(s_s)
        c_s[...] = jnp.zeros_like(c_s)
        n_s[...] = jnp.zeros_like(n_s)
        m_s[...] = jnp.zeros_like(m_s)
        qkbuf[0:SUBLANES, :] = jnp.zeros((SUBLANES, qkbuf.shape[1]), F32)
        _SideWork(_proj_pieces(x_ref, 0, tile, g_ref, nbuf, win_ref, zb0, 2 * PROJ_PIECE)).flush()

    mix = functools.partial(_odd_tile, dmask_ref=dmask_ref, cw_ref=cw_ref, cb_ref=cb_ref, gb_ref=gb_ref,
                            s_s=s_s, c_s=c_s, n_s=n_s, m_s=m_s, qkbuf=qkbuf, tile=tile)
    side = _SideWork(_proj_pieces(x_ref, tile, 2 * tile, g_ref, nbuf, win_ref, zb1, 2 * PROJ_PIECE),
                     ODD_TILE_STAGES)

    def rope(t):
        cb, sb = rbase_ref[0, 2 * t:2 * t + 1, :], rbase_ref[0, 2 * t + 1:2 * t + 2, :]
        return (cb * roff_ref[0] - sb * roff_ref[1], sb * roff_ref[2] + cb * roff_ref[3])

    mix(zb0, yb0, *rope(0), side)
    side.flush()
    side = _SideWork(_proj_pieces(xn_ref, 0, tile, g_ref, nbuf, win_ref, zb0, 2 * PROJ_PIECE)
                     + _out_pieces(yb0, wout_ref, x_ref, 0, o_ref, 0, tile, PROJ_PIECE), ODD_TILE_STAGES)
    mix(zb1, yb1, *rope(1), side)
    side.flush()
    _SideWork(_out_pieces(yb1, wout_ref, x_ref, tile, o_ref, tile, tile, PROJ_PIECE)).flush()
    conv_ref[...] = qkbuf[0:SUBLANES, :]
    so_ref[...] = s_s[...]
    co_ref[...] = c_s[...]
    no_ref[...] = n_s[...]
    mo_ref[...] = m_s[...]


def _odd_prompt(x, g, w_in, w_out, rbase, roff, dmask, cw, cb, gb):
    n, d = x.shape
    zin = w_in.shape[1]
    tile = dmask.shape[1]
    assert n % (2 * tile) == 0
    nt = n // tile
    wy = H_C * DV_C + H_D * DV_D
    wqk = 2 * H_D * DK_D
    small = (dmask, cw, cb, gb)
    return pl.pallas_call(
        functools.partial(_odd_prompt_body, tile=tile),
        grid=(nt // 2,),
        in_specs=[pl.BlockSpec((2 * tile, d), lambda i: (i, 0)),
                  pl.BlockSpec((tile, d), lambda i: (jnp.minimum(2 * i + 2, nt - 1), 0)),
                  _resident(g.shape), _resident(w_in.shape), _resident(w_out.shape),
                  pl.BlockSpec((1, 4, DK_C), lambda i: (i, 0, 0)), _resident(roff.shape)]
                 + [_resident(a.shape) for a in small],
        out_specs=[pl.BlockSpec((2 * tile, d), lambda i: (i, 0)),
                   pl.BlockSpec((SUBLANES, wqk), lambda i: (0, 0)),
                   pl.BlockSpec((H_C, DK_C, DV_C), lambda i: (0, 0, 0)),
                   pl.BlockSpec((H_D, DK_D, DV_D), lambda i: (0, 0, 0)),
                   pl.BlockSpec((SUBLANES, DK_D), lambda i: (0, 0)),
                   pl.BlockSpec((SUBLANES, LANES), lambda i: (0, 0))],
        out_shape=[jax.ShapeDtypeStruct((n, d), F32),
                   jax.ShapeDtypeStruct((SUBLANES, wqk), F32),
                   jax.ShapeDtypeStruct((H_C, DK_C, DV_C), F32),
                   jax.ShapeDtypeStruct((H_D, DK_D, DV_D), F32),
                   jax.ShapeDtypeStruct((SUBLANES, DK_D), F32),
                   jax.ShapeDtypeStruct((SUBLANES, LANES), F32)],
        scratch_shapes=[pltpu.VMEM((tile, zin), F32), pltpu.VMEM((tile, zin), F32),
                        pltpu.VMEM((tile, wy), BF16), pltpu.VMEM((tile, wy), BF16),
                        pltpu.VMEM((tile, d), BF16),
                        pltpu.VMEM((H_C, DK_C, DV_C), F32), pltpu.VMEM((H_D, DK_D, DV_D), F32),
                        pltpu.VMEM((SUBLANES, DK_D), F32), pltpu.VMEM((SUBLANES, LANES), F32),
                        pltpu.VMEM((tile + SUBLANES, wqk), F32)],
        compiler_params=_params(), name="odd_prompt")(x, x, g, w_in, w_out, rbase, roff, *small)


def _odd_sample_body(z_ref, cos_ref, sin_ref, s_ref, conv_ref, c_ref, n_ref, m_ref, cw_ref, cb_ref, gb_ref,
                     y_ref, so_ref, cvo_ref, co_ref, no_ref, mo_ref, xc, *, sb, dl):
    wq = H_C * DK_C
    wv = H_C * DV_C
    o_kc, o_vc, o_gc = wq, 2 * wq, 2 * wq + wv
    o_qk = 2 * wq + 2 * wv
    wqk = 2 * H_D * DK_D
    o_vd = o_qk + wqk
    o_od = o_vd + H_D * DV_D
    o_gt = o_od + H_D * DV_D
    lg = _retention_log_decay()
    rows = sb * dl

    def z3(c0, width):
        return z_ref[:, c0:c0 + width].reshape(sb, dl, width)

    cosf = cos_ref[...][None]
    sinf = sin_ref[...][None]
    t1 = lax.broadcasted_iota(jnp.int32, (1, dl, 1), 1)
    tf = t1.astype(F32)

    for h in range(H_C):
        q = _rotate(z3(h * DK_C, DK_C), cosf, sinf, 2)
        k = _rotate(z3(o_kc + h * DK_C, DK_C), cosf, sinf, 2) * (DK_C ** -0.5)
        v = z3(o_vc + h * DV_C, DV_C)
        xi = jnp.exp((tf + 1.0) * lg[h])
        zeta = jnp.exp((dl - 1.0 - tf) * lg[h])
        s0 = s_ref[:, h]
        o = jnp.einsum('bqd,bde->bqe', (q * xi).astype(BF16), s0.astype(BF16), preferred_element_type=F32)
        for s in range(dl):
            ks = k if s == 0 else pltpu.roll(k, s, axis=1)
            vs = v if s == 0 else pltpu.roll(v, s, axis=1)
            coef = jnp.sum(q * ks, axis=-1, keepdims=True) * math.exp(s * lg[h])
            o = o + jnp.where(t1 >= s, coef, 0.0) * vs
        upd = jnp.einsum('btd,bte->bde', (k * zeta).astype(BF16), v.astype(BF16), preferred_element_type=F32)
        so_ref[:, h] = math.exp(dl * lg[h]) * s0 + upd
        gate = z3(o_gc + h * DV_C, DV_C)
        y_ref[:, h * DV_C:(h + 1) * DV_C] = (o * _rms_scale(o) * _silu(gate)).reshape(rows, DV_C).astype(BF16)

    xqk3 = z3(o_qk, wqk)
    xc[:, SUBLANES:SUBLANES + dl, :] = xqk3
    xc[:, SUBLANES - (CONV_W - 1):SUBLANES, :] = conv_ref[...]
    cvo_ref[...] = xqk3[:, dl - (CONV_W - 1):dl, :]
    cw = cw_ref[...]
    qk = cb_ref[...] + cw[3:4] * xqk3
    for i in range(CONV_W - 1):
        off = SUBLANES - (CONV_W - 1) + i
        qk = qk + cw[i:i + 1] * xc[:, off:off + dl, :]
    qk = _silu(qk)

    gates = z3(o_gt, LANES) + gb_ref[...]
    logf = _log_sigmoid(gates)
    t = lax.broadcasted_iota(jnp.int32, gates.shape, 1)
    bsum = logf
    s = 1
    while s < dl:
        bsum = bsum + jnp.where(t >= s, pltpu.roll(bsum, s, axis=1), 0.0)
        s *= 2
    bsum = pltpu.roll(bsum, LANES - H_D, axis=2)
    m0 = m_ref[...]
    init_log = bsum + m0
    m_t = init_log
    dlogs = []
    for s in range(dl):
        if s == 0:
            d = gates
        else:
            d = bsum - pltpu.roll(bsum, s, axis=1) + pltpu.roll(gates, s, axis=1)
        d = jnp.where(t >= s, d, NEG)
        dlogs.append(d)
        m_t = jnp.maximum(m_t, d)
    a0 = jnp.exp(init_log - m_t)
    ws = [jnp.exp(d - m_t) for d in dlogs]
    inv_floor = jnp.exp(-m_t)
    b_end = bsum[:, dl - 1:dl, :]
    log_end = b_end - bsum + gates
    m_new = jnp.maximum(b_end + m0, jnp.max(log_end, axis=1, keepdims=True))
    w_end = jnp.exp(log_end - m_new)
    a_end = jnp.exp(b_end + m0 - m_new)
    mo_ref[...] = m_new
    for h in range(H_D):
        q = qk[:, :, h * DK_D:(h + 1) * DK_D]
        k = qk[:, :, H_D * DK_D + h * DK_D:H_D * DK_D + (h + 1) * DK_D] * (DK_D ** -0.5)
        v = z3(o_vd + h * DV_D, DV_D)
        c0 = c_ref[:, h]
        n0 = n_ref[:, h:h + 1, :]
        a0h = a0[:, :, h:h + 1]
        num = a0h * jnp.einsum('bqd,bde->bqe', q.astype(BF16), c0.astype(BF16), preferred_element_type=F32)
        den = a0h * jnp.sum(q * n0, axis=-1, keepdims=True)
        for s in range(dl):
            ks = k if s == 0 else pltpu.roll(k, s, axis=1)
            vs = v if s == 0 else pltpu.roll(v, s, axis=1)
            coef = jnp.sum(q * ks, axis=-1, keepdims=True) * ws[s][:, :, h:h + 1]
            num = num + coef * vs
            den = den + coef
        den = jnp.maximum(jnp.abs(den), inv_floor[:, :, h:h + 1])
        og = z3(o_od + h * DV_D, DV_D)
        y_ref[:, wv + h * DV_D:wv + (h + 1) * DV_D] = (
            (num / den) * _sigmoid(og)).reshape(rows, DV_D).astype(BF16)
        kw = k * w_end[:, :, h:h + 1]
        aeh = a_end[:, :, h:h + 1]
        upd = jnp.einsum('btd,bte->bde', kw.astype(BF16), v.astype(BF16), preferred_element_type=F32)
        co_ref[:, h] = aeh * c0 + upd
        no_ref[:, h:h + 1, :] = aeh * n0 + jnp.sum(kw, axis=1, keepdims=True)


def _odd_sample(z, cosf, sinf, s0, conv, c0, n0, m0, cw, cb, gb, *, dl):
    n, zin = z.shape
    db = n // dl
    sb = min(ODD_SB, db)
    assert db % sb == 0 and dl == SUBLANES
    rows = sb * dl
    wy = H_C * DV_C + H_D * DV_D
    wqk = 2 * H_D * DK_D
    st_spec = pl.BlockSpec((sb, H_C, DK_C, DV_C), lambda i: (i, 0, 0, 0))
    n_spec = pl.BlockSpec((sb, H_D, DK_D), lambda i: (i, 0, 0))
    m_spec = pl.BlockSpec((sb, 1, LANES), lambda i: (i, 0, 0))
    return pl.pallas_call(
        functools.partial(_odd_sample_body, sb=sb, dl=dl),
        grid=(db // sb,),
        in_specs=[pl.BlockSpec((rows, zin), lambda i: (i, 0)), _resident(cosf.shape), _resident(sinf.shape),
                  st_spec, pl.BlockSpec((sb, CONV_W - 1, wqk), lambda i: (i, 0, 0)), st_spec, n_spec, m_spec]
                 + [_resident(a.shape) for a in (cw, cb, gb)],
        out_specs=[pl.BlockSpec((rows, wy), lambda i: (i, 0)), st_spec,
                   pl.BlockSpec((sb, CONV_W - 1, wqk), lambda i: (i, 0, 0)), st_spec, n_spec, m_spec],
        out_shape=[jax.ShapeDtypeStruct((n, wy), BF16),
                   jax.ShapeDtypeStruct(s0.shape, F32), jax.ShapeDtypeStruct(conv.shape, F32),
                   jax.ShapeDtypeStruct(c0.shape, F32),
                   jax.ShapeDtypeStruct(n0.shape, F32), jax.ShapeDtypeStruct(m0.shape, F32)],
        scratch_shapes=[pltpu.VMEM((sb, 2 * SUBLANES, wqk), F32)],
        compiler_params=_params(), name="odd_sample")(z, cosf, sinf, s0, conv, c0, n0, m0, cw, cb, gb)


def _t5_bucket(dist):
    n = np.maximum(dist, 0)
    max_exact = N_BUCKETS // 2
    large = max_exact + (np.log(np.maximum(n, max_exact) / max_exact)
                         / math.log(MAX_DIST / max_exact) * (N_BUCKETS - max_exact)).astype(np.int32)
    return np.where(n < max_exact, n, np.minimum(large, N_BUCKETS - 1)).astype(np.int32)


def _attn_bias(rel_bias, nq_rows, nkeys, p0):
    rel = np.arange(nq_rows)[:, None] + WINDOW - np.arange(nkeys)[None, :]
    kpos_ok = (p0 - WINDOW + np.arange(nkeys)) >= 0
    mask = (rel >= 0) & (rel < WINDOW) & kpos_ok[None, :]
    onehot = jnp.asarray(_t5_bucket(rel)[:, :, None] == np.arange(N_BUCKETS), F32)
    bias = jnp.einsum('qkb,bh->hqk', onehot, rel_bias.astype(F32), precision=lax.Precision.HIGHEST)
    bias = jnp.where(mask[None], bias, NEG)
    return bias.reshape(KV_B, G_B * nq_rows, nkeys)


def _sink_rows(sinks, nq_rows, width=1):
    col = jnp.repeat(sinks.astype(F32).reshape(KV_B, G_B), nq_rows, axis=1).reshape(KV_B, G_B * nq_rows, 1)
    return jnp.broadcast_to(col, (KV_B, G_B * nq_rows, width))


def _rope_tables(p0, n):
    half = DK_C // 2
    pos = p0 + jnp.arange(n, dtype=F32)
    inv = ROPE_BASE ** (-jnp.arange(half, dtype=F32) / half)
    ang = pos[:, None] * inv[None, :]
    cos, sin = jnp.cos(ang), jnp.sin(ang)
    return jnp.concatenate([cos, cos], axis=1), jnp.concatenate([-sin, sin], axis=1)


def _rope_split_tables(n, tile):
    half = DK_C // 2
    inv = ROPE_BASE ** (-jnp.arange(half, dtype=F32) / half)
    dup = lambda a: jnp.concatenate([a, a], axis=-1)
    base = (jnp.arange(n // tile, dtype=F32) * tile)[:, None] * inv[None, :]
    rbase = jnp.stack([dup(jnp.cos(base)), dup(jnp.sin(base))], axis=1).reshape(n // (2 * tile), 4, DK_C)
    off = jnp.arange(tile, dtype=F32)[:, None] * inv[None, :]
    sign = jnp.concatenate([-jnp.ones((half,), F32), jnp.ones((half,), F32)])
    co, so = dup(jnp.cos(off)), dup(jnp.sin(off))
    return rbase, jnp.stack([co, so, co * sign, so * sign])


def _decay_mask(tile):
    t = np.arange(tile)
    diff = (t[:, None] - t[None, :]).astype(np.float32)
    lg = np.asarray(_retention_log_decay(), np.float32)
    return jnp.asarray(np.where(diff >= 0, np.exp(np.maximum(diff, 0)[None] * lg[:, None, None]), 0.0), F32)


def _block_diag_gates(w_r, w_i, group):
    nb, c, _ = w_r.shape
    eye = jnp.eye(group, dtype=w_r.dtype)

    def bd(w):
        w4 = w.reshape(nb // group, group, c, c)
        return jnp.einsum('gncd,nm->gncmd', w4, eye).reshape(nb // group, group * c, group * c)

    return jnp.concatenate([bd(w_r), bd(w_i)], axis=2).astype(BF16)


def kernel(x_prompt, x_sample, state_a_conv, state_a_h, cache_b_k, cache_b_v, state_c_S, state_d_conv, state_d_C, state_d_n, state_d_m, norm_g, ffn1_w_in, ffn1_w_out, ffn2_w_in, ffn2_w_out, even_w_in, even_w_out, a_conv_w, a_conv_b, a_w_r, a_b_r, a_w_i, a_b_i, a_lambda, b_qk_norm, b_sinks, rel_bias, odd_w_in, odd_w_out, d_conv_w, d_conv_b, d_gate_b):
    bp, lp, d = x_prompt.shape
    db, dl, _ = x_sample.shape
    assert bp == 1, "prompt group is a single sequence"
    depth = norm_g.shape[0]
    w_a = a_conv_w.shape[2]
    nkv = KV_B * HD_B
    wqk = 2 * H_D * DK_D
    tile = min(MIX_TILE, lp // 2)

    xp = x_prompt.reshape(lp, d)
    xs = x_sample.reshape(db * dl, d)

    bias_p = _attn_bias(rel_bias, WINDOW, 2 * WINDOW, WINDOW)
    bias_s = _attn_bias(rel_bias, dl, WINDOW + dl, PAST_LEN)
    rbase_p, roff_p = _rope_split_tables(lp, tile)
    cos_s, sin_s = _rope_tables(float(PAST_LEN), dl)
    dmask = _decay_mask(tile)

    st = {}
    pending = []
    first_block = None
    for li in range(depth):
        j = li // 2
        if first_block is None:
            xs, wg, wu, wo = _ffn_cast(xs, norm_g[li, 0], ffn1_w_in, ffn1_w_out, li)
        else:
            wg, wu, wo = first_block
            first_block = None
        if li % 2 == 1:
            xp = _ffn(xp, *pending, (norm_g[li, 0], wg, wu, wo))
            pending = []
        if li % 2 == 0:
            zs, w_in = _norm_proj_cast(xs, norm_g[li, 1], even_w_in, j, EVEN_IN_CHUNK)
            shared = (a_conv_w[j], a_conv_b[j].reshape(1, w_a),
                      _block_diag_gates(a_w_r[j], a_w_i[j], 4),
                      jnp.stack([a_b_r[j], a_b_i[j]]), a_lambda[j].reshape(1, w_a),
                      b_qk_norm[j, 0].reshape(1, HD_B), jnp.tile(b_qk_norm[j, 1], KV_B).reshape(1, nkv))
            ys, conv_s, h_s, k_s, v_s = _even_sample(
                zs, state_a_conv[j], state_a_h[j], cache_b_k[j].reshape(db, WINDOW, nkv),
                cache_b_v[j].reshape(db, WINDOW, nkv), *shared, bias_s, _sink_rows(b_sinks[j], dl), dl=dl)
            xs, w_out = _out_proj_cast(ys, even_w_out, j, xs)
            xp, conv_p, h_p, k_p, v_p = _even_prompt(
                xp, norm_g[li, 0].reshape(1, d), wg, wu, wo,
                norm_g[li, 1].reshape(1, d), w_in, w_out, *shared[:5],
                jnp.tile(b_qk_norm[j, 0], H_B).reshape(1, H_B * HD_B), shared[6],
                bias_p, _sink_rows(b_sinks[j], WINDOW, LANES))
            st.setdefault('a_conv', ([], []))
            st['a_conv'][0].append(conv_p[SUBLANES - (CONV_W - 1):].reshape(1, CONV_W - 1, w_a))
            st['a_conv'][1].append(conv_s)
            st.setdefault('a_h', ([], []))
            st['a_h'][0].append(h_p[0:1])
            st['a_h'][1].append(h_s.reshape(db, w_a))
            st.setdefault('b_k', ([], []))
            st['b_k'][0].append(k_p.reshape(1, WINDOW, KV_B, HD_B))
            st['b_k'][1].append(k_s.reshape(db, WINDOW, KV_B, HD_B))
            st.setdefault('b_v', ([], []))
            st['b_v'][0].append(v_p.reshape(1, WINDOW, KV_B, HD_B))
            st['b_v'][1].append(v_s.reshape(db, WINDOW, KV_B, HD_B))
        else:
            zs, w_in = _norm_proj_cast(xs, norm_g[li, 1], odd_w_in, j, ODD_IN_CHUNK)
            gb = jnp.pad(d_gate_b[j], (0, LANES - 2 * H_D)).reshape(1, LANES)
            shared = (d_conv_w[j], d_conv_b[j].reshape(1, wqk), gb)
            m0 = jnp.pad(state_d_m[j], ((0, 0), (0, LANES - H_D))).reshape(db, 1, LANES)
            ys, s_s, dconv_s, c_s, n_s, m_s = _odd_sample(zs, cos_s, sin_s, state_c_S[j], state_d_conv[j],
                                                          state_d_C[j], state_d_n[j], m0, *shared, dl=dl)
            xs, w_out = _out_proj_cast(ys, odd_w_out, j, xs)
            xp, dconv_p, s_p, c_p, n_p, m_p = _odd_prompt(xp, norm_g[li, 1].reshape(1, d), w_in, w_out,
                                                          rbase_p, roff_p, dmask, *shared)
            st.setdefault('c_S', ([], []))
            st['c_S'][0].append(s_p[None])
            st['c_S'][1].append(s_s)
            st.setdefault('d_conv', ([], []))
            st['d_conv'][0].append(dconv_p[SUBLANES - (CONV_W - 1):].reshape(1, CONV_W - 1, wqk))
            st['d_conv'][1].append(dconv_s)
            st.setdefault('d_C', ([], []))
            st['d_C'][0].append(c_p[None])
            st['d_C'][1].append(c_s)
            st.setdefault('d_n', ([], []))
            st['d_n'][0].append(n_p[None, :H_D])
            st['d_n'][1].append(n_s)
            st.setdefault('d_m', ([], []))
            st['d_m'][0].append(m_p[:H_D, 0].reshape(1, H_D))
            st['d_m'][1].append(m_s[:, 0, :H_D])
        if li % 2 == 0 and li + 1 < depth:
            xs, (wg, wu, wo), first_block = _ffn_cast2(xs, norm_g[li, 2], ffn2_w_in, ffn2_w_out, li,
                                                       norm_g[li + 1, 0], ffn1_w_in, ffn1_w_out, li + 1)
        else:
            xs, wg, wu, wo = _ffn_cast(xs, norm_g[li, 2], ffn2_w_in, ffn2_w_out, li)
        pending.append((norm_g[li, 2], wg, wu, wo))
        if li % 2 == 1 or li == depth - 1:
            xp = _ffn(xp, *pending)
            pending = []

    outs = [xp.reshape(1, lp, d), xs.reshape(db, dl, d)]
    for name in ('a_conv', 'a_h', 'b_k', 'b_v', 'c_S', 'd_conv', 'd_C', 'd_n', 'd_m'):
        outs.append(jnp.stack(st[name][0]))
        outs.append(jnp.stack(st[name][1]))
    return tuple(outs)
```
